```python
import math, functools
import jax, jax.numpy as jnp
from jax import lax
import numpy as np

D_MODEL = 1024
BATCH = 32
SEQ = 2048
DEPTH = 1
DEC_BATCH = 128
DEC_SEQ = 8
PAST_LEN = 16384
PAGE_SIZE = 128

MLA_HEADS = 8
MLA_NOPE = 64
MLA_ROPE = 32
MLA_V = 64
MLA_Q_LORA = 384
MLA_KV_LORA = 256
MLA_SCALE = (MLA_NOPE + MLA_ROPE) ** -0.5
MOBA_HEADS = 8
MOBA_HD = 64
MOBA_WIDTH = MOBA_HEADS * MOBA_HD
MOBA_BLOCK = 256
MOBA_TOPK = 3
MOBA_SCALE = MOBA_HD ** -0.5
PAGES_PER_BLOCK = MOBA_BLOCK // PAGE_SIZE
ROT_DIMS = MOBA_HD // 4
ROPE_THETA = 500000.0
PEER_HEADS = 8
PEER_NKEYS = 128
PEER_N = PEER_NKEYS * PEER_NKEYS
PEER_DKEY = 256
PEER_TOPK = 16
ATTN_Q_BLOCK = 128
MOBA_Q_CHUNK = 16
PEER_CHUNK = 256
EPS = 1e-6
IN_SIZES = (MLA_Q_LORA, MLA_KV_LORA, MLA_ROPE, MOBA_WIDTH, MOBA_WIDTH, MOBA_WIDTH, D_MODEL, D_MODEL)
IN_COLS = MLA_Q_LORA + MLA_KV_LORA + MLA_ROPE + 3 * MOBA_WIDTH + 2 * D_MODEL

kernel_name = 'hybrid_mla_moba_peer_adaln_step'


def rmsnorm(x, g):
    xf = x.astype(jnp.float32)
    y = xf * lax.rsqrt(jnp.mean(xf * xf, axis=-1, keepdims=True) + EPS)
    return (y * g).astype(x.dtype)


def rope(x, pos, rot):
    half = rot // 2
    inv_freq = jnp.exp(-math.log(ROPE_THETA) * jnp.arange(half, dtype=jnp.float32) / half)
    ang = pos.astype(jnp.float32)[:, None] * inv_freq[None, :]
    cos = jnp.cos(ang)[:, None, :]
    sin = jnp.sin(ang)[:, None, :]
    xr = x[..., :rot].astype(jnp.float32)
    x1, x2 = xr[..., :half], xr[..., half:]
    rotated = jnp.concatenate([x1 * cos - x2 * sin, x2 * cos + x1 * sin], axis=-1).astype(x.dtype)
    if rot == x.shape[-1]:
        return rotated
    return jnp.concatenate([rotated, x[..., rot:]], axis=-1)


def split_cols(z):
    outs, start = [], 0
    for size in IN_SIZES:
        outs.append(z[..., start:start + size])
        start += size
    return outs


def adaln(c, w_ada, b_ada):
    mod = jax.nn.silu(c) @ w_ada + b_ada
    return jnp.split(mod[:, None, :], 6, axis=-1)


def mla_logits(q_lat, q_rope, lat, kpe):
    s = jnp.einsum('bqhc,bkc->bhqk', q_lat, lat) + jnp.einsum('bqhr,bkr->bhqk', q_rope, kpe)
    return s.astype(jnp.float32) * MLA_SCALE


def mla_attend_prompt(q_lat, q_rope, kv_lat, k_pe):
    B, T, H, C = q_lat.shape
    kpos = jnp.arange(T)

    def block(i):
        q0 = i * ATTN_Q_BLOCK
        ql = lax.dynamic_slice_in_dim(q_lat, q0, ATTN_Q_BLOCK, axis=1)
        qr = lax.dynamic_slice_in_dim(q_rope, q0, ATTN_Q_BLOCK, axis=1)
        qpos = q0 + jnp.arange(ATTN_Q_BLOCK)
        s = jnp.where(kpos[None, :] <= qpos[:, None], mla_logits(ql, qr, kv_lat, k_pe), -jnp.inf)
        p = jax.nn.softmax(s, axis=-1).astype(kv_lat.dtype)
        return jnp.einsum('bhqk,bkc->bqhc', p, kv_lat)

    o = lax.map(block, jnp.arange(T // ATTN_Q_BLOCK))
    return o.transpose(1, 0, 2, 3, 4).reshape(B, T, H, C)


def mla_attend_sample(q_lat, q_rope, kv_lat, k_pe, lat_past, kpe_past):
    Q = q_lat.shape[1]
    P = lat_past.shape[1]
    s_past = mla_logits(q_lat, q_rope, lat_past, kpe_past)
    causal = jnp.arange(Q)[None, :] <= jnp.arange(Q)[:, None]
    s_new = jnp.where(causal, mla_logits(q_lat, q_rope, kv_lat, k_pe), -jnp.inf)
    p = jax.nn.softmax(jnp.concatenate([s_past, s_new], axis=-1), axis=-1).astype(lat_past.dtype)
    return (jnp.einsum('bhqk,bkc->bqhc', p[..., :P], lat_past)
            + jnp.einsum('bhqk,bkc->bqhc', p[..., P:], kv_lat))


def moba_attend(qc, k_own, v_own, own_mask, ks, vs, valid):
    s_own = jnp.where(own_mask, jnp.einsum('bqhd,bkhd->bqhk', qc, k_own).astype(jnp.float32) * MOBA_SCALE, -jnp.inf)
    if ks is None:
        p = jax.nn.softmax(s_own, axis=-1).astype(v_own.dtype)
        return jnp.einsum('bqhk,bkhd->bqhd', p, v_own)
    s_past = jnp.einsum('bqhd,bqhskd->bqhsk', qc, ks).astype(jnp.float32) * MOBA_SCALE
    if valid is not None:
        s_past = jnp.where(valid, s_past, -jnp.inf)
    B, Q, H, S, BS = s_past.shape
    n_past = S * BS
    p = jax.nn.softmax(jnp.concatenate([s_past.reshape(B, Q, H, n_past), s_own], axis=-1), axis=-1).astype(v_own.dtype)
    return (jnp.einsum('bqhsk,bqhskd->bqhd', p[..., :n_past].reshape(B, Q, H, S, BS), vs)
            + jnp.einsum('bqhk,bkhd->bqhd', p[..., n_past:], v_own))


def moba_attend_prompt(q, k, v):
    B, T, H, Dh = q.shape
    nblk = -(-T // MOBA_BLOCK)
    pad = nblk * MOBA_BLOCK - T
    kb = jnp.pad(k, ((0, 0), (0, pad), (0, 0), (0, 0))).reshape(B, nblk, MOBA_BLOCK, H, Dh)
    vb = jnp.pad(v, ((0, 0), (0, pad), (0, 0), (0, 0))).reshape(B, nblk, MOBA_BLOCK, H, Dh)
    kmean = jnp.mean(kb, axis=2, dtype=jnp.float32)
    ksel = min(MOBA_TOPK, nblk - 1)
    bi = jnp.arange(B)[:, None, None, None]
    hi = jnp.arange(H)[None, None, :, None]

    def chunk(i):
        q0 = i * MOBA_Q_CHUNK
        qc = lax.dynamic_slice_in_dim(q, q0, MOBA_Q_CHUNK, axis=1)
        qpos = q0 + jnp.arange(MOBA_Q_CHUNK)
        blk = q0 // MOBA_BLOCK
        k_own = lax.dynamic_index_in_dim(kb, blk, axis=1, keepdims=False)
        v_own = lax.dynamic_index_in_dim(vb, blk, axis=1, keepdims=False)
        kpos = blk * MOBA_BLOCK + jnp.arange(MOBA_BLOCK)
        own_mask = (kpos[None, :] <= qpos[:, None])[None, :, None, :]
        if ksel == 0:
            return moba_attend(qc, k_own, v_own, own_mask, None, None, None)
        n_full = qpos // MOBA_BLOCK
        gate = jnp.einsum('bqhd,bnhd->bqhn', qc, kmean).astype(jnp.float32)
        gate = jnp.where((jnp.arange(nblk)[None, :] < n_full[:, None])[None, :, None, :], gate, -jnp.inf)
        _, sel = lax.top_k(gate, ksel)
        valid = (jnp.arange(ksel)[None, :] < jnp.minimum(n_full, MOBA_TOPK)[:, None])[None, :, None, :, None]
        ks = kb[bi, sel, :, hi]
        vs = vb[bi, sel, :, hi]
        return moba_attend(qc, k_own, v_own, own_mask, ks, vs, valid)

    o = lax.map(chunk, jnp.arange(T // MOBA_Q_CHUNK))
    return o.transpose(1, 0, 2, 3, 4).reshape(B, T, H, Dh)


def moba_attend_sample(q, k, v, cache_k, cache_v, page_mean_k, page_table, layer):
    B, Q, H, Dh = q.shape
    n_full = PAST_LEN // MOBA_BLOCK
    n_own_past = PAST_LEN - n_full * MOBA_BLOCK
    own_pages = page_table[:, n_full * PAGES_PER_BLOCK: PAST_LEN // PAGE_SIZE]
    k_own = jnp.concatenate([cache_k[layer, own_pages].reshape(B, n_own_past, H, Dh), k], axis=1)
    v_own = jnp.concatenate([cache_v[layer, own_pages].reshape(B, n_own_past, H, Dh), v], axis=1)
    kpos_rel = jnp.arange(n_own_past + Q)
    ksel = min(MOBA_TOPK, n_full)
    if ksel > 0:
        kmean = page_mean_k[page_table[:, :n_full * PAGES_PER_BLOCK]].reshape(B, n_full, PAGES_PER_BLOCK, H, Dh).mean(axis=2)
    bi = jnp.arange(B)[:, None, None, None, None]
    hi = jnp.arange(H)[None, None, :, None, None]

    def one(i):
        qc = lax.dynamic_slice_in_dim(q, i, 1, axis=1)
        own_mask = (kpos_rel <= n_own_past + i)[None, None, None, :]
        if ksel == 0:
            return moba_attend(qc, k_own, v_own, own_mask, None, None, None)
        gate = jnp.einsum('bqhd,bnhd->bqhn', qc, kmean).astype(jnp.float32)
        _, sel = lax.top_k(gate, ksel)
        phys = page_table[bi, sel[..., None] * PAGES_PER_BLOCK + jnp.arange(PAGES_PER_BLOCK)]
        ks = cache_k[layer, phys, :, hi].reshape(B, 1, H, ksel, MOBA_BLOCK, Dh)
        vs = cache_v[layer, phys, :, hi].reshape(B, 1, H, ksel, MOBA_BLOCK, Dh)
        return moba_attend(qc, k_own, v_own, own_mask, ks, vs, None)

    o = lax.map(one, jnp.arange(Q))
    return o[:, :, 0].transpose(1, 0, 2, 3)


def peer(x, w_pq, sub_keys, u_tab, v_tab):
    B, T, D = x.shape
    n = B * T
    pad = (-n) % PEER_CHUNK
    xt = jnp.pad(x.reshape(n, D), ((0, pad), (0, 0))).reshape(-1, PEER_CHUNK, D)
    K = PEER_TOPK

    def chunk(xc):
        q = (xc @ w_pq).reshape(PEER_CHUNK, PEER_HEADS, 2, PEER_DKEY // 2)
        s = jnp.einsum('chpk,hpnk->chpn', q, sub_keys).astype(jnp.float32)
        s_top, i_top = lax.top_k(s, K)
        cand = (s_top[:, :, 0, :, None] + s_top[:, :, 1, None, :]).reshape(PEER_CHUNK, PEER_HEADS, K * K)
        cand_idx = (i_top[:, :, 0, :, None] * PEER_NKEYS + i_top[:, :, 1, None, :]).reshape(PEER_CHUNK, PEER_HEADS, K * K)
        best, pos = lax.top_k(cand, K)
        eidx = jnp.take_along_axis(cand_idx, pos, axis=-1)
        g = jax.nn.softmax(best, axis=-1)
        u = u_tab[eidx]
        v = v_tab[eidx]
        a = jax.nn.gelu(jnp.einsum('chkd,cd->chk', u, xc).astype(jnp.float32), approximate=False)
        return jnp.einsum('chk,chkd->cd', (g * a).astype(v.dtype), v)

    y = lax.map(chunk, xt)
    return y.reshape(-1, D)[:n].reshape(B, T, D)


def trunk_layer(x, c, pos, mla_attend, moba_attend_fn, w_ada, b_ada, g_mix, g_ffn, w_in, g_q, w_uq, g_kv, w_uk, w_uv,
                w_o_mla, w_o_moba, w_out, w_pq, sub_keys, u_tab, v_tab):
    B, T, _ = x.shape
    sh_a, sc_a, gt_a, sh_f, sc_f, gt_f = adaln(c, w_ada, b_ada)
    h = rmsnorm(x, g_mix) * (1.0 + sc_a) + sh_a
    q_down, kv_down, k_rope_raw, q_mb, k_mb, v_mb, gate_mla, gate_moba = split_cols(h @ w_in)
    q = jnp.einsum('btr,rhd->bthd', rmsnorm(q_down, g_q), w_uq)
    q_nope = q[..., :MLA_NOPE]
    q_rope = rope(q[..., MLA_NOPE:], pos, MLA_ROPE)
    kv_lat = rmsnorm(kv_down, g_kv)
    k_pe = rope(k_rope_raw[:, :, None, :], pos, MLA_ROPE)[:, :, 0, :]
    q_lat = jnp.einsum('bthd,hcd->bthc', q_nope, w_uk)
    o_lat = mla_attend(q_lat, q_rope, kv_lat, k_pe)
    o_mla = jnp.einsum('bthc,hcv->bthv', o_lat, w_uv).reshape(B, T, MLA_HEADS * MLA_V) @ w_o_mla
    q_mb = rope(q_mb.reshape(B, T, MOBA_HEADS, MOBA_HD), pos, ROT_DIMS)
    k_mb = rope(k_mb.reshape(B, T, MOBA_HEADS, MOBA_HD), pos, ROT_DIMS)
    v_mb = v_mb.reshape(B, T, MOBA_HEADS, MOBA_HD)
    o_mb = moba_attend_fn(q_mb, k_mb, v_mb).reshape(B, T, MOBA_WIDTH) @ w_o_moba
    merged = jax.nn.sigmoid(gate_mla) * o_mla + jax.nn.sigmoid(gate_moba) * o_mb
    x = x + gt_a * (merged @ w_out)
    hf = rmsnorm(x, g_ffn) * (1.0 + sc_f) + sh_f
    x = x + gt_f * peer(hf, w_pq, sub_keys, u_tab, v_tab)
    return x, (kv_lat, k_pe, k_mb, v_mb)


def setup_inputs(seed: int = 0) -> dict:
    key = jax.random.key(seed)
    ks = jax.random.split(key, 32)
    f32 = jnp.float32
    n_pages = PAST_LEN // PAGE_SIZE
    n_used = DEC_BATCH * n_pages
    n_pool = n_used + (n_used + 3) // 4

    def nrm(k, shape, scale):
        return jax.random.normal(k, shape, f32) * scale

    def gain(k, shape):
        return 1.0 + 0.01 * jax.random.normal(k, shape, f32)

    page_table = jax.random.permutation(ks[0], n_pool)[:n_used].reshape(DEC_BATCH, n_pages).astype(jnp.int32)
    return {
        'x_prompt': nrm(ks[1], (BATCH, SEQ, D_MODEL), 1.0),
        'x_sample': nrm(ks[2], (DEC_BATCH, DEC_SEQ, D_MODEL), 1.0),
        'cache_mla_latent': nrm(ks[3], (DEPTH, n_pool, PAGE_SIZE, MLA_KV_LORA), 1.0),
        'cache_mla_kpe': nrm(ks[4], (DEPTH, n_pool, PAGE_SIZE, MLA_ROPE), 1.0),
        'cache_moba_k': nrm(ks[5], (DEPTH, n_pool, PAGE_SIZE, MOBA_HEADS, MOBA_HD), 1.0),
        'cache_moba_v': nrm(ks[6], (DEPTH, n_pool, PAGE_SIZE, MOBA_HEADS, MOBA_HD), 1.0),
        'page_table': page_table,
        'c_prompt': nrm(ks[7], (BATCH, D_MODEL), 1.0),
        'c_sample': nrm(ks[8], (DEC_BATCH, D_MODEL), 1.0),
        'w_ada': nrm(ks[9], (DEPTH, D_MODEL, 6 * D_MODEL), 0.5 * D_MODEL ** -0.5),
        'b_ada': nrm(ks[10], (DEPTH, 6 * D_MODEL), 0.01),
        'g_norm_mix': gain(ks[11], (DEPTH, D_MODEL)),
        'g_norm_ffn': gain(ks[12], (DEPTH, D_MODEL)),
        'w_in': nrm(ks[13], (DEPTH, D_MODEL, IN_COLS), D_MODEL ** -0.5),
        'g_q_lora': gain(ks[14], (DEPTH, MLA_Q_LORA)),
        'w_uq': nrm(ks[15], (DEPTH, MLA_Q_LORA, MLA_HEADS, MLA_NOPE + MLA_ROPE), MLA_Q_LORA ** -0.5),
        'g_kv_lora': gain(ks[16], (DEPTH, MLA_KV_LORA)),
        'w_uk': nrm(ks[17], (DEPTH, MLA_HEADS, MLA_KV_LORA, MLA_NOPE), MLA_KV_LORA ** -0.5),
        'w_uv': nrm(ks[18], (DEPTH, MLA_HEADS, MLA_KV_LORA, MLA_V), MLA_KV_LORA ** -0.5),
        'w_o_mla': nrm(ks[19], (DEPTH, MLA_HEADS * MLA_V, D_MODEL), (MLA_HEADS * MLA_V) ** -0.5),
        'w_o_moba': nrm(ks[20], (DEPTH, MOBA_WIDTH, D_MODEL), MOBA_WIDTH ** -0.5),
        'w_out': nrm(ks[21], (DEPTH, D_MODEL, D_MODEL), D_MODEL ** -0.5),
        'w_peer_q': nrm(ks[22], (DEPTH, D_MODEL, PEER_HEADS * PEER_DKEY), D_MODEL ** -0.5),
        'peer_sub_keys': nrm(ks[23], (DEPTH, PEER_HEADS, 2, PEER_NKEYS, PEER_DKEY // 2), (PEER_DKEY // 2) ** -0.5),
        'peer_u': nrm(ks[24], (DEPTH, PEER_N, D_MODEL), D_MODEL ** -0.5),
        'peer_v': nrm(ks[25], (DEPTH, PEER_N, D_MODEL), 1.0),
        'g_norm_final': gain(ks[26], (D_MODEL,)),
    }


def reference(x_prompt, x_sample, cache_mla_latent, cache_mla_kpe, cache_moba_k, cache_moba_v, page_table,
              c_prompt, c_sample, w_ada, b_ada, g_norm_mix, g_norm_ffn, w_in, g_q_lora, w_uq, g_kv_lora, w_uk, w_uv,
              w_o_mla, w_o_moba, w_out, w_peer_q, peer_sub_keys, peer_u, peer_v, g_norm_final):
    n_dec = x_sample.shape[0]
    pos_p = jnp.arange(x_prompt.shape[1], dtype=jnp.int32)
    pos_s = PAST_LEN + jnp.arange(x_sample.shape[1], dtype=jnp.int32)
    page_mean_k = jnp.mean(cache_moba_k, axis=2, dtype=jnp.float32)
    hp, hs = x_prompt, x_sample
    new_p, new_s = [], []
    for l in range(DEPTH):
        lw = (w_ada[l], b_ada[l], g_norm_mix[l], g_norm_ffn[l], w_in[l], g_q_lora[l], w_uq[l], g_kv_lora[l],
              w_uk[l], w_uv[l], w_o_mla[l], w_o_moba[l], w_out[l], w_peer_q[l], peer_sub_keys[l], peer_u[l], peer_v[l])
        hp, st_p = trunk_layer(hp, c_prompt, pos_p, mla_attend_prompt, moba_attend_prompt, *lw)
        lat_past = cache_mla_latent[l, page_table].reshape(n_dec, PAST_LEN, MLA_KV_LORA)
        kpe_past = cache_mla_kpe[l, page_table].reshape(n_dec, PAST_LEN, MLA_ROPE)
        mla_s = functools.partial(mla_attend_sample, lat_past=lat_past, kpe_past=kpe_past)
        moba_s = functools.partial(moba_attend_sample, cache_k=cache_moba_k, cache_v=cache_moba_v,
                                   page_mean_k=page_mean_k[l], page_table=page_table, layer=l)
        hs, st_s = trunk_layer(hs, c_sample, pos_s, mla_s, moba_s, *lw)
        new_p.append(st_p)
        new_s.append(st_s)
    y_prompt = rmsnorm(hp, g_norm_final)
    y_sample = rmsnorm(hs, g_norm_final)
    p_lat = jnp.stack([s[0] for s in new_p])
    p_kpe = jnp.stack([s[1] for s in new_p])
    p_k = jnp.stack([s[2] for s in new_p])
    p_v = jnp.stack([s[3] for s in new_p])
    s_lat = jnp.stack([s[0] for s in new_s])
    s_kpe = jnp.stack([s[1] for s in new_s])
    s_k = jnp.stack([s[2] for s in new_s])
    s_v = jnp.stack([s[3] for s in new_s])
    return (y_prompt, y_sample, p_lat, p_kpe, p_k, p_v, s_lat, s_kpe, s_k, s_v)
```

```python
import functools
import math

import jax
import jax.numpy as jnp
from jax import lax
from jax.experimental import pallas as pl
from jax.experimental.pallas import tpu as pltpu

F32 = jnp.float32
BF16 = jnp.bfloat16
I32 = jnp.int32

D_MODEL = 1024
PAGE_SIZE = 128
MLA_HEADS = 8
MLA_NOPE = 64
MLA_ROPE = 32
MLA_V = 64
MLA_Q_LORA = 384
MLA_KV_LORA = 256
MLA_SCALE = (MLA_NOPE + MLA_ROPE) ** -0.5
MOBA_HEADS = 8
MOBA_HD = 64
MOBA_WIDTH = MOBA_HEADS * MOBA_HD
MOBA_BLOCK = 256
MOBA_TOPK = 3
MOBA_SCALE = MOBA_HD ** -0.5
PAGES_PER_BLOCK = MOBA_BLOCK // PAGE_SIZE
ROT_DIMS = MOBA_HD // 4
ROPE_THETA = 500000.0
PEER_HEADS = 8
PEER_NKEYS = 128
PEER_DKEY = 256
PEER_TOPK = 16
EPS = 1e-6

LANES = 128
NEG = -1e30
NT_DIMS = (((1,), (1,)), ((), ()))

OFF_QD = 0
OFF_KV = OFF_QD + MLA_Q_LORA
OFF_KR = OFF_KV + MLA_KV_LORA
OFF_QM = OFF_KR + LANES
OFF_KM = OFF_QM + MOBA_WIDTH
OFF_VM = OFF_KM + MOBA_WIDTH
OFF_GT = OFF_VM + MOBA_WIDTH
IN_COLS_P = OFF_GT + 2 * D_MODEL
KR_LANE = MLA_NOPE

TOK_TILE = 256
ATT_TILE = 256
PEER_TOK = 8
VMEM_LIMIT = 56 * 1024 * 1024


def _cparams(sem):
    return pltpu.CompilerParams(dimension_semantics=sem, vmem_limit_bytes=VMEM_LIMIT)


def _rms(x, g):
    return x * lax.rsqrt(jnp.mean(x * x, axis=-1, keepdims=True) + EPS) * g


def _rope_lanes(z, t_ref, half):
    c, a, b = t_ref[0], t_ref[1], t_ref[2]
    outs = []
    for j in range(z.shape[1] // LANES):
        zj = z[:, j * LANES:(j + 1) * LANES]
        outs.append(zj * c + pltpu.roll(zj, LANES - half, 1) * a + pltpu.roll(zj, half, 1) * b)
    return outs[0] if len(outs) == 1 else jnp.concatenate(outs, axis=1)


def _online_update(s, m_ref, l_ref, acc_ref, v, idx=None):
    m_prev = m_ref[...] if idx is None else m_ref[idx]
    l_prev = l_ref[...] if idx is None else l_ref[idx]
    a_prev = acc_ref[...] if idx is None else acc_ref[idx]
    m_new = jnp.maximum(m_prev, jnp.max(s, axis=1, keepdims=True))
    alpha = jnp.exp(m_prev - m_new)
    p = jnp.exp(s - m_new)
    l_new = alpha * l_prev + jnp.sum(p, axis=1, keepdims=True)
    a_new = alpha * a_prev + jnp.dot(p.astype(BF16), v, preferred_element_type=F32)
    if idx is None:
        m_ref[...], l_ref[...], acc_ref[...] = m_new, l_new, a_new
    else:
        m_ref[idx], l_ref[idx], acc_ref[idx] = m_new, l_new, a_new


def _adaln_kernel(c_ref, w_ref, b_ref, o_ref):
    c = c_ref[...]
    s = c * jax.nn.sigmoid(c)
    o_ref[...] = jnp.dot(s.astype(BF16), w_ref[...], preferred_element_type=F32) + b_ref[...]


def _adaln(c, w, b):
    nb, d = c.shape
    n = w.shape[1]
    tn = n // 4
    return pl.pallas_call(
        _adaln_kernel,
        grid=(n // tn,),
        in_specs=[pl.BlockSpec((nb, d), lambda j: (0, 0)),
                  pl.BlockSpec((d, tn), lambda j: (0, j)),
                  pl.BlockSpec((1, tn), lambda j: (0, j))],
        out_specs=pl.BlockSpec((nb, tn), lambda j: (0, j)),
        out_shape=jax.ShapeDtypeStruct((nb, n), F32),
        compiler_params=_cparams(("arbitrary",)),
        name="adaln",
    )(c, w, b)


def _xspec(tt, w):
    return pl.BlockSpec((1, tt, w), lambda g, t: (g, t, 0))


def _modspec(mod, tt):
    if mod.shape[1] == 1:
        return pl.BlockSpec((1, 1, mod.shape[2]), lambda g, t: (g, 0, 0))
    return pl.BlockSpec((1, tt, mod.shape[2]), lambda g, t: (g, t, 0))


def _wspec(shape):
    nd = len(shape)
    return pl.BlockSpec(shape, lambda g, t: (0,) * nd)


def _tabspec(tt):
    return pl.BlockSpec((3, tt, LANES), lambda g, t: (0, t, 0))


def _inproj_kernel(x_ref, sh_ref, sc_ref, gm_ref, w_ref, gq_ref, gkv_ref, tq_ref, tm_ref,
                   qn_ref, kv_ref, kpe_ref, qmb_ref, kmb_ref, vmb_ref, sg_ref, km_ref):
    x = x_ref[0]
    h = _rms(x, gm_ref[...]) * (1.0 + sc_ref[0]) + sh_ref[0]
    hb = h.astype(BF16)

    def seg(off, n):
        return jnp.dot(hb, w_ref[:, off:off + n], preferred_element_type=F32)

    qn_ref[0] = _rms(seg(OFF_QD, MLA_Q_LORA), gq_ref[...]).astype(BF16)
    kv_ref[0] = _rms(seg(OFF_KV, MLA_KV_LORA), gkv_ref[...])
    kpe_ref[0] = _rope_lanes(seg(OFF_KR, LANES), tq_ref, MLA_ROPE // 2)
    qmb_ref[0] = _rope_lanes(seg(OFF_QM, MOBA_WIDTH), tm_ref, ROT_DIMS // 2).astype(BF16)
    kmb = _rope_lanes(seg(OFF_KM, MOBA_WIDTH), tm_ref, ROT_DIMS // 2)
    kmb_ref[0] = kmb
    km_ref[0, 0] = jnp.mean(kmb, axis=0, keepdims=True)
    vmb_ref[0] = seg(OFF_VM, MOBA_WIDTH)
    sg_ref[0] = jax.nn.sigmoid(seg(OFF_GT, 2 * D_MODEL)).astype(BF16)


def _inproj(x, sh, sc, gm, w, gq, gkv, tq, tm):
    g, tg, d = x.shape
    tt = min(TOK_TILE, tg)
    nt = tg // tt
    outs = [(MLA_Q_LORA, BF16), (MLA_KV_LORA, F32), (LANES, F32), (MOBA_WIDTH, BF16),
            (MOBA_WIDTH, F32), (MOBA_WIDTH, F32), (2 * D_MODEL, BF16)]
    return pl.pallas_call(
        _inproj_kernel,
        grid=(g, nt),
        in_specs=[_xspec(tt, d), _modspec(sh, tt), _modspec(sc, tt), _wspec(gm.shape), _wspec(w.shape),
                  _wspec(gq.shape), _wspec(gkv.shape), _tabspec(tt), _tabspec(tt)],
        out_specs=[_xspec(tt, n) for n, _ in outs]
        + [pl.BlockSpec((1, 1, 1, MOBA_WIDTH), lambda gi, t: (gi, t, 0, 0))],
        out_shape=[jax.ShapeDtypeStruct((g, tg, n), dt) for n, dt in outs]
        + [jax.ShapeDtypeStruct((g, nt, 1, MOBA_WIDTH), F32)],
        compiler_params=_cparams(("parallel", "arbitrary")),
        name="inproj",
    )(x, sh, sc, gm, w, gq, gkv, tq, tm)


def _qproj_kernel(qn_ref, w_ref, tq_ref, q_ref):
    q = jnp.dot(qn_ref[0], w_ref[...], preferred_element_type=F32)
    q_ref[0] = _rope_lanes(q, tq_ref, MLA_ROPE // 2).astype(BF16)


def _qproj(qn, w, tq):
    g, tg, r = qn.shape
    tt = min(TOK_TILE, tg)
    n = w.shape[1]
    return pl.pallas_call(
        _qproj_kernel,
        grid=(g, tg // tt),
        in_specs=[_xspec(tt, r), _wspec(w.shape), _tabspec(tt)],
        out_specs=_xspec(tt, n),
        out_shape=jax.ShapeDtypeStruct((g, tg, n), BF16),
        compiler_params=_cparams(("parallel", "arbitrary")),
        name="mla_qproj",
    )(qn, w, tq)


def _qlat_kernel(q_ref, w_ref, o_ref):
    for h in range(MLA_HEADS):
        qh = q_ref[0, :, h * LANES:(h + 1) * LANES]
        o_ref[0, :, h * MLA_KV_LORA:(h + 1) * MLA_KV_LORA] = jnp.dot(
            qh, w_ref[h], preferred_element_type=F32).astype(BF16)


def _qlat(q, wukt):
    g, tg, n = q.shape
    tt = min(TOK_TILE, tg)
    no = MLA_HEADS * MLA_KV_LORA
    return pl.pallas_call(
        _qlat_kernel,
        grid=(g, tg // tt),
        in_specs=[_xspec(tt, n), _wspec(wukt.shape)],
        out_specs=_xspec(tt, no),
        out_shape=jax.ShapeDtypeStruct((g, tg, no), BF16),
        compiler_params=_cparams(("parallel", "arbitrary")),
        name="mla_qlat",
    )(q, wukt)


def _kvprep_kernel(kv_ref, kpe_ref, wuk_ref, wuv_ref, k_ref, v_ref):
    lat = kv_ref[0].astype(BF16)
    kn = jnp.dot(lat, wuk_ref[...], preferred_element_type=F32)
    kpe = kpe_ref[0]
    k_ref[0] = (kn + jnp.concatenate([kpe] * MLA_HEADS, axis=1)).astype(BF16)
    v_ref[0] = jnp.dot(lat, wuv_ref[...], preferred_element_type=F32).astype(BF16)


def _kvprep(kvlat, kpe, wuk, wuv):
    g, tg, c = kvlat.shape
    tt = min(TOK_TILE, tg)
    return pl.pallas_call(
        _kvprep_kernel,
        grid=(g, tg // tt),
        in_specs=[_xspec(tt, c), _xspec(tt, LANES), _wspec(wuk.shape), _wspec(wuv.shape)],
        out_specs=[_xspec(tt, wuk.shape[1]), _xspec(tt, wuv.shape[1])],
        out_shape=[jax.ShapeDtypeStruct((g, tg, wuk.shape[1]), BF16),
                   jax.ShapeDtypeStruct((g, tg, wuv.shape[1]), BF16)],
        compiler_params=_cparams(("parallel", "arbitrary")),
        name="mla_kvprep",
    )(kvlat, kpe, wuk, wuv)


def _mla_prompt_kernel(q_ref, k_ref, v_ref, o_ref, m_sc, l_sc, acc_sc):
    qi, ki = pl.program_id(1), pl.program_id(2)
    tq = q_ref.shape[1]

    @pl.when(ki == 0)
    def _():
        m_sc[...] = jnp.full(m_sc.shape, NEG, F32)
        l_sc[...] = jnp.zeros(l_sc.shape, F32)
        acc_sc[...] = jnp.zeros(acc_sc.shape, F32)

    @pl.when(ki <= qi)
    def _():
        row = lax.broadcasted_iota(I32, (tq, tq), 0)
        col = lax.broadcasted_iota(I32, (tq, tq), 1)
        keep = jnp.logical_or(col <= row, ki < qi)
        for h in range(MLA_HEADS):
            q = q_ref[0, :, h * LANES:(h + 1) * LANES]
            k = k_ref[0, :, h * LANES:(h + 1) * LANES]
            s = lax.dot_general(q, k, NT_DIMS, preferred_element_type=F32) * MLA_SCALE
            s = jnp.where(keep, s, NEG)
            vp = v_ref[0, :, (h // 2) * LANES:(h // 2 + 1) * LANES]
            _online_update(s, m_sc, l_sc, acc_sc, vp, idx=h)

    @pl.when(ki == qi)
    def _():
        outs = []
        for h in range(MLA_HEADS):
            o = acc_sc[h] / l_sc[h]
            outs.append(o[:, (h % 2) * MLA_V:(h % 2 + 1) * MLA_V])
        o_ref[0] = jnp.concatenate(outs, axis=1).astype(o_ref.dtype)


def _mla_prompt(q, k, v):
    b, t, _ = q.shape
    ta = min(ATT_TILE, t)
    nq = t // ta
    return pl.pallas_call(
        _mla_prompt_kernel,
        grid=(b, nq, nq),
        in_specs=[pl.BlockSpec((1, ta, q.shape[2]), lambda bi, qi, ki: (bi, qi, 0)),
                  pl.BlockSpec((1, ta, k.shape[2]), lambda bi, qi, ki: (bi, jnp.minimum(ki, qi), 0)),
                  pl.BlockSpec((1, ta, v.shape[2]), lambda bi, qi, ki: (bi, jnp.minimum(ki, qi), 0))],
        out_specs=pl.BlockSpec((1, ta, v.shape[2]), lambda bi, qi, ki: (bi, qi, 0)),
        out_shape=jax.ShapeDtypeStruct((b, t, v.shape[2]), BF16),
        scratch_shapes=[pltpu.VMEM((MLA_HEADS, ta, 1), F32), pltpu.VMEM((MLA_HEADS, ta, 1), F32),
                        pltpu.VMEM((MLA_HEADS, ta, LANES), F32)],
        compiler_params=_cparams(("parallel", "arbitrary", "arbitrary")),
        name="mla_prompt_attn",
    )(q, k, v)


def _moba_prompt_kernel(q_ref, k_ref, v_ref, km_ref, o_ref, bias_sc, m_sc, l_sc, acc_sc, *, nblk):
    qi, ki = pl.program_id(1), pl.program_id(2)
    tq = q_ref.shape[1]
    lane = lax.broadcasted_iota(I32, (tq, LANES), 1)

    def head_q(h):
        qp = q_ref[0, :, (h // 2) * LANES:(h // 2 + 1) * LANES]
        return jnp.where((lane // MOBA_HD) == (h % 2), qp, jnp.zeros_like(qp))

    @pl.when(ki == 0)
    def _():
        m_sc[...] = jnp.full(m_sc.shape, NEG, F32)
        l_sc[...] = jnp.zeros(l_sc.shape, F32)
        acc_sc[...] = jnp.zeros(acc_sc.shape, F32)
        for h in range(MOBA_HEADS):
            kmp = km_ref[0, :, (h // 2) * LANES:(h // 2 + 1) * LANES]
            gate = lax.dot_general(head_q(h).astype(F32), kmp, NT_DIMS, preferred_element_type=F32,
                                   precision=lax.Precision.HIGHEST)
            gate = jnp.where(lane < qi, gate, -jnp.inf)
            rank = jnp.zeros((tq, LANES), I32)
            for i in range(nblk):
                c = gate[:, i:i + 1]
                beats = jnp.logical_or(c > gate, jnp.logical_and(c == gate, lane > i))
                rank = rank + beats.astype(I32)
            sel = jnp.logical_and(rank < MOBA_TOPK, lane < qi)
            bias_sc[h] = jnp.where(sel, 0.0, NEG)

    def attend(mask_fn):
        for h in range(MOBA_HEADS):
            kp = k_ref[0, :, (h // 2) * LANES:(h // 2 + 1) * LANES].astype(BF16)
            s = lax.dot_general(head_q(h), kp, NT_DIMS, preferred_element_type=F32) * MOBA_SCALE
            vp = v_ref[0, :, (h // 2) * LANES:(h // 2 + 1) * LANES].astype(BF16)
            _online_update(mask_fn(s, h), m_sc, l_sc, acc_sc, vp, idx=h)

    @pl.when(ki < qi)
    def _():
        def past(s, h):
            colb = jnp.sum(jnp.where(lane == ki, bias_sc[h], 0.0), axis=1, keepdims=True)
            return s + colb
        attend(past)

    @pl.when(ki == qi)
    def _():
        row = lax.broadcasted_iota(I32, (tq, tq), 0)
        col = lax.broadcasted_iota(I32, (tq, tq), 1)
        attend(lambda s, h: jnp.where(col <= row, s, NEG))
        outs = []
        for h in range(MOBA_HEADS):
            o = acc_sc[h] / l_sc[h]
            outs.append(o[:, (h % 2) * MOBA_HD:(h % 2 + 1) * MOBA_HD])
        o_ref[0] = jnp.concatenate(outs, axis=1).astype(o_ref.dtype)


def _moba_prompt(q, k, v, kmean):
    b, t, w = q.shape
    assert t % MOBA_BLOCK == 0
    nblk = t // MOBA_BLOCK
    tb = MOBA_BLOCK
    return pl.pallas_call(
        functools.partial(_moba_prompt_kernel, nblk=nblk),
        grid=(b, nblk, nblk),
        in_specs=[pl.BlockSpec((1, tb, w), lambda bi, qi, ki: (bi, qi, 0)),
                  pl.BlockSpec((1, tb, w), lambda bi, qi, ki: (bi, jnp.minimum(ki, qi), 0)),
                  pl.BlockSpec((1, tb, w), lambda bi, qi, ki: (bi, jnp.minimum(ki, qi), 0)),
                  pl.BlockSpec((1, LANES, w), lambda bi, qi, ki: (bi, 0, 0))],
        out_specs=pl.BlockSpec((1, tb, w), lambda bi, qi, ki: (bi, qi, 0)),
        out_shape=jax.ShapeDtypeStruct((b, t, w), BF16),
        scratch_shapes=[pltpu.VMEM((MOBA_HEADS, tb, LANES), F32), pltpu.VMEM((MOBA_HEADS, tb, 1), F32),
                        pltpu.VMEM((MOBA_HEADS, tb, 1), F32), pltpu.VMEM((MOBA_HEADS, tb, LANES), F32)],
        compiler_params=_cparams(("parallel", "arbitrary", "arbitrary")),
        name="moba_prompt_attn",
    )(q, k, v, kmean)


MLA_PAGES_PER_STEP = 4


def _mla_sample_kernel(pt_ref, ql_ref, qr_ref, *rest):
    npp = MLA_PAGES_PER_STEP
    lat_refs, kpe_refs = rest[:npp], rest[npp:2 * npp]
    latn_ref, kpen_ref, wuv_ref, o_ref, m_sc, l_sc, acc_sc = rest[2 * npp:]
    step = pl.program_id(1)
    rows = ql_ref.shape[1]
    nq = rows // MLA_HEADS

    @pl.when(step == 0)
    def _():
        m_sc[...] = jnp.full(m_sc.shape, NEG, F32)
        l_sc[...] = jnp.zeros(l_sc.shape, F32)
        acc_sc[...] = jnp.zeros(acc_sc.shape, F32)

    ql, qr = ql_ref[0], qr_ref[0]

    def scores(lat, kpe):
        s = lax.dot_general(ql, lat, NT_DIMS, preferred_element_type=F32)
        s = s + lax.dot_general(qr, kpe, NT_DIMS, preferred_element_type=F32)
        return s * MLA_SCALE

    for j in range(npp):
        lat = lat_refs[j][0].astype(BF16)
        kpe = kpe_refs[j][0].astype(BF16)
        _online_update(scores(lat, kpe), m_sc, l_sc, acc_sc, lat)

    @pl.when(step == pl.num_programs(1) - 1)
    def _():
        latn = latn_ref[0].astype(BF16)
        s = scores(latn, kpen_ref[0].astype(BF16))
        row = lax.broadcasted_iota(I32, s.shape, 0)
        col = lax.broadcasted_iota(I32, s.shape, 1)
        s = jnp.where(col <= (row % nq), s, NEG)
        _online_update(s, m_sc, l_sc, acc_sc, latn)
        on = (acc_sc[...] / l_sc[...]).astype(BF16)
        t = jnp.dot(on, wuv_ref[...], preferred_element_type=F32)
        r2 = lax.broadcasted_iota(I32, t.shape, 0)
        c2 = lax.broadcasted_iota(I32, t.shape, 1)
        t = jnp.where((c2 // MLA_V) == (r2 // nq), t, 0.0)
        out = t[0:nq]
        for h in range(1, MLA_HEADS):
            out = out + t[h * nq:(h + 1) * nq]
        o_ref[0] = out.astype(o_ref.dtype)


def _mla_sample(page_table, qlat, qrope, lat_pool, kpe_pool, lat_new, kpe_new, wuv):
    b, rows, c = qlat.shape
    n_pages = page_table.shape[1]
    npp = MLA_PAGES_PER_STEP
    assert n_pages % npp == 0
    nq = rows // MLA_HEADS

    def page_spec(width, j):
        return pl.BlockSpec((1, PAGE_SIZE, width), lambda bi, s, pt: (pt[bi, s * npp + j], 0, 0))

    def bspec(shape):
        return pl.BlockSpec((1,) + shape, lambda bi, s, pt: (bi, 0, 0))

    grid_spec = pltpu.PrefetchScalarGridSpec(
        num_scalar_prefetch=1,
        grid=(b, n_pages // npp),
        in_specs=[bspec((rows, c)), bspec((rows, MLA_ROPE))]
        + [page_spec(c, j) for j in range(npp)] + [page_spec(MLA_ROPE, j) for j in range(npp)]
        + [bspec(lat_new.shape[1:]), bspec(kpe_new.shape[1:]),
           pl.BlockSpec(wuv.shape, lambda bi, s, pt: (0, 0))],
        out_specs=pl.BlockSpec((1, nq, wuv.shape[1]), lambda bi, s, pt: (bi, 0, 0)),
        scratch_shapes=[pltpu.VMEM((rows, 1), F32), pltpu.VMEM((rows, 1), F32), pltpu.VMEM((rows, c), F32)],
    )
    return pl.pallas_call(
        _mla_sample_kernel,
        grid_spec=grid_spec,
        out_shape=jax.ShapeDtypeStruct((b, nq, wuv.shape[1]), BF16),
        compiler_params=_cparams(("parallel", "arbitrary")),
        name="mla_sample_attn",
    )(page_table, qlat, qrope, *([lat_pool] * npp), *([kpe_pool] * npp), lat_new, kpe_new, wuv)


def _blockmean_kernel(pt_ref, *refs):
    k_refs, o_ref = refs[:PAGES_PER_BLOCK], refs[PAGES_PER_BLOCK]
    acc = None
    for r in k_refs:
        pm = jnp.sum(r[0], axis=0) * (1.0 / PAGE_SIZE)
        acc = pm if acc is None else acc + pm
    o_ref[0, 0] = acc * (1.0 / PAGES_PER_BLOCK)


def _blockmean(page_table, k_pool, nblk):
    b = page_table.shape[0]
    ppb = PAGES_PER_BLOCK

    def page_spec(j):
        return pl.BlockSpec((1, PAGE_SIZE, MOBA_HEADS, MOBA_HD), lambda bi, n, pt: (pt[bi, n * ppb + j], 0, 0, 0))

    grid_spec = pltpu.PrefetchScalarGridSpec(
        num_scalar_prefetch=1,
        grid=(b, nblk),
        in_specs=[page_spec(j) for j in range(ppb)],
        out_specs=pl.BlockSpec((1, 1, MOBA_HEADS, MOBA_HD), lambda bi, n, pt: (bi, n, 0, 0)),
    )
    return pl.pallas_call(
        _blockmean_kernel,
        grid_spec=grid_spec,
        out_shape=jax.ShapeDtypeStruct((b, nblk, MOBA_HEADS, MOBA_HD), F32),
        compiler_params=_cparams(("parallel", "arbitrary")),
        name="moba_blockmean",
    )(page_table, *([k_pool] * ppb))


def _moba_sample_kernel(pt_ref, q_ref, km_ref, *rest, nblk, nq):
    ppb = PAGES_PER_BLOCK
    k_refs, v_refs = rest[:ppb], rest[ppb:2 * ppb]
    kn_ref, vn_ref, o_ref, bias_sc, m_sc, l_sc, acc_sc = rest[2 * ppb:]
    n = pl.program_id(1)
    rows = q_ref.shape[1]
    ncol = nblk * MOBA_HEADS
    q = q_ref[0]

    @pl.when(n == 0)
    def _():
        m_sc[...] = jnp.full(m_sc.shape, NEG, F32)
        l_sc[...] = jnp.zeros(l_sc.shape, F32)
        acc_sc[...] = jnp.zeros(acc_sc.shape, F32)
        kmr = km_ref[0].reshape(ncol, MOBA_HD)
        gate = lax.dot_general(q.astype(F32), kmr, NT_DIMS, preferred_element_type=F32,
                               precision=lax.Precision.HIGHEST)
        row = lax.broadcasted_iota(I32, gate.shape, 0)
        col = lax.broadcasted_iota(I32, gate.shape, 1)
        gate = jnp.where((col % MOBA_HEADS) == (row // nq), gate, -jnp.inf)
        chosen = jnp.zeros(gate.shape, jnp.bool_)
        for _ in range(min(MOBA_TOPK, nblk)):
            mx = jnp.max(gate, axis=1, keepdims=True)
            first = jnp.min(jnp.where(gate == mx, col, ncol), axis=1, keepdims=True)
            hit = col == first
            chosen = jnp.logical_or(chosen, hit)
            gate = jnp.where(hit, -jnp.inf, gate)
        bias_sc[...] = jnp.where(chosen, 0.0, NEG)

    bcol = lax.broadcasted_iota(I32, bias_sc.shape, 1)
    colb = jnp.max(jnp.where((bcol // MOBA_HEADS) == n, bias_sc[...], NEG), axis=1, keepdims=True)
    nkeys = PAGE_SIZE * MOBA_HEADS
    prow = lax.broadcasted_iota(I32, (rows, nkeys), 0)
    pcol = lax.broadcasted_iota(I32, (rows, nkeys), 1)
    own_head = (pcol % MOBA_HEADS) == (prow // nq)
    for j in range(ppb):
        kp = k_refs[j][0].reshape(nkeys, MOBA_HD).astype(BF16)
        vp = v_refs[j][0].reshape(nkeys, MOBA_HD).astype(BF16)
        s = lax.dot_general(q, kp, NT_DIMS, preferred_element_type=F32) * MOBA_SCALE
        s = jnp.where(own_head, s + colb, NEG)
        _online_update(s, m_sc, l_sc, acc_sc, vp)

    @pl.when(n == nblk - 1)
    def _():
        kn = kn_ref[0].astype(BF16)
        s = lax.dot_general(q, kn, NT_DIMS, preferred_element_type=F32) * MOBA_SCALE
        r = lax.broadcasted_iota(I32, s.shape, 0)
        c = lax.broadcasted_iota(I32, s.shape, 1)
        ok = jnp.logical_and((c % MOBA_HEADS) == (r // nq), (c // MOBA_HEADS) <= (r % nq))
        s = jnp.where(ok, s, NEG)
        _online_update(s, m_sc, l_sc, acc_sc, vn_ref[0].astype(BF16))
        o_ref[0] = acc_sc[...] / l_sc[...]


def _moba_sample(page_table, q, kmean, k_pool, v_pool, k_new, v_new):
    b, rows, hd = q.shape
    nblk = kmean.shape[1]
    nq = rows // MOBA_HEADS
    ppb = PAGES_PER_BLOCK

    def page_spec(j):
        return pl.BlockSpec((1, PAGE_SIZE, MOBA_HEADS, MOBA_HD), lambda bi, n, pt: (pt[bi, n * ppb + j], 0, 0, 0))

    def bspec(shape):
        nd = len(shape)
        return pl.BlockSpec((1,) + shape, lambda bi, n, pt: (bi,) + (0,) * nd)

    grid_spec = pltpu.PrefetchScalarGridSpec(
        num_scalar_prefetch=1,
        grid=(b, nblk),
        in_specs=[bspec((rows, hd)), bspec(kmean.shape[1:])]
        + [page_spec(j) for j in range(ppb)] + [page_spec(j) for j in range(ppb)]
        + [bspec(k_new.shape[1:]), bspec(v_new.shape[1:])],
        out_specs=bspec((rows, hd)),
        scratch_shapes=[pltpu.VMEM((rows, nblk * MOBA_HEADS), F32), pltpu.VMEM((rows, 1), F32),
                        pltpu.VMEM((rows, 1), F32), pltpu.VMEM((rows, hd), F32)],
    )
    return pl.pallas_call(
        functools.partial(_moba_sample_kernel, nblk=nblk, nq=nq),
        grid_spec=grid_spec,
        out_shape=jax.ShapeDtypeStruct((b, rows, hd), F32),
        compiler_params=_cparams(("parallel", "arbitrary")),
        name="moba_sample_attn",
    )(page_table, q, kmean, *([k_pool] * ppb), *([v_pool] * ppb), k_new, v_new)


def _post_kernel(om_ref, ob_ref, sg_ref, x_ref, gt_ref, sc_ref, sh_ref, wom_ref, wob_ref, wout_ref,
                 gf_ref, wpq_ref, x1_ref, hf_ref, qp_ref):
    o_mla = jnp.dot(om_ref[0], wom_ref[...], preferred_element_type=F32)
    o_mb = jnp.dot(ob_ref[0], wob_ref[...], preferred_element_type=F32)
    sg = sg_ref[0].astype(F32)
    merged = sg[:, :D_MODEL] * o_mla + sg[:, D_MODEL:] * o_mb
    y = jnp.dot(merged.astype(BF16), wout_ref[...], preferred_element_type=F32)
    x1 = x_ref[0] + gt_ref[0] * y
    x1_ref[0] = x1
    hf = _rms(x1, gf_ref[...]) * (1.0 + sc_ref[0]) + sh_ref[0]
    hf_ref[0] = hf
    qp_ref[0] = jnp.dot(hf.astype(BF16), wpq_ref[...], preferred_element_type=F32).astype(BF16)


def _post(om, ob, sg, x, gt, sc, sh, wom, wob, wout, gf, wpq):
    g, tg, d = x.shape
    tt = min(TOK_TILE, tg)
    nq = wpq.shape[1]
    return pl.pallas_call(
        _post_kernel,
        grid=(g, tg // tt),
        in_specs=[_xspec(tt, om.shape[2]), _xspec(tt, ob.shape[2]), _xspec(tt, sg.shape[2]), _xspec(tt, d),
                  _modspec(gt, tt), _modspec(sc, tt), _modspec(sh, tt),
                  _wspec(wom.shape), _wspec(wob.shape), _wspec(wout.shape), _wspec(gf.shape), _wspec(wpq.shape)],
        out_specs=[_xspec(tt, d), _xspec(tt, d), _xspec(tt, nq)],
        out_shape=[jax.ShapeDtypeStruct((g, tg, d), F32), jax.ShapeDtypeStruct((g, tg, d), F32),
                   jax.ShapeDtypeStruct((g, tg, nq), BF16)],
        compiler_params=_cparams(("parallel", "arbitrary")),
        name="post_attn",
    )(om, ob, sg, x, gt, sc, sh, wom, wob, wout, gf, wpq)


def _topk_rows(s, k, val_sc, idx_sc, payload=None):
    nrow = s.shape[0]
    rid = lax.broadcasted_iota(I32, s.shape, 0)

    def body(r, s):
        mx = jnp.max(s, axis=0, keepdims=True)
        first = jnp.min(jnp.where(s == mx, rid, nrow), axis=0, keepdims=True)
        hit = rid == first
        val_sc[pl.ds(r, 1), :] = mx
        if payload is None:
            idx_sc[pl.ds(r, 1), :] = first
        else:
            idx_sc[pl.ds(r, 1), :] = jnp.max(jnp.where(hit, payload, -1), axis=0, keepdims=True)
        return jnp.where(hit, -jnp.inf, s)

    lax.fori_loop(0, k, body, s)


def _router_kernel(qp_ref, sk_ref, eid_ref, g_ref, v1_sc, i1_sc, v2_sc, i2_sc, vb_sc, ib_sc, e_sc, w_sc):
    kk = PEER_TOPK
    half = PEER_DKEY // 2
    for h in range(PEER_HEADS):
        tops = []
        for p, (v_sc, i_sc) in enumerate(((v1_sc, i1_sc), (v2_sc, i2_sc))):
            qhp = qp_ref[:, (2 * h + p) * half:(2 * h + p + 1) * half]
            s = lax.dot_general(sk_ref[h, p], qhp, NT_DIMS, preferred_element_type=F32)
            _topk_rows(s, kk, v_sc, i_sc)
            tops.append((v_sc[...], i_sc[...]))
        (a, ia), (b, ib) = tops
        cand = jnp.concatenate([a[i:i + 1, :] + b for i in range(kk)], axis=0)
        cidx = jnp.concatenate([ia[i:i + 1, :] * PEER_NKEYS + ib for i in range(kk)], axis=0)
        _topk_rows(cand, kk, vb_sc, ib_sc, payload=cidx)
        best = vb_sc[...]
        ex = jnp.exp(best - jnp.max(best, axis=0, keepdims=True))
        w_sc[h * kk:(h + 1) * kk, :] = ex / jnp.sum(ex, axis=0, keepdims=True)
        e_sc[h * kk:(h + 1) * kk, :] = ib_sc[...]
    eid_ref[...] = e_sc[...].T
    g_ref[...] = w_sc[...].T


def _router(qp, sk):
    n, w = qp.shape
    tt = min(TOK_TILE, n)
    kk = PEER_TOPK
    nsel = PEER_HEADS * kk
    return pl.pallas_call(
        _router_kernel,
        grid=(n // tt,),
        in_specs=[pl.BlockSpec((tt, w), lambda i: (i, 0)),
                  pl.BlockSpec(sk.shape, lambda i: (0, 0, 0, 0))],
        out_specs=[pl.BlockSpec((tt, nsel), lambda i: (i, 0)), pl.BlockSpec((tt, nsel), lambda i: (i, 0))],
        out_shape=[jax.ShapeDtypeStruct((n, nsel), I32), jax.ShapeDtypeStruct((n, nsel), F32)],
        scratch_shapes=[pltpu.VMEM((kk, tt), F32), pltpu.VMEM((kk, tt), I32),
                        pltpu.VMEM((kk, tt), F32), pltpu.VMEM((kk, tt), I32),
                        pltpu.VMEM((kk, tt), F32), pltpu.VMEM((kk, tt), I32),
                        pltpu.VMEM((nsel, tt), I32), pltpu.VMEM((nsel, tt), F32)],
        compiler_params=_cparams(("parallel",)),
        name="peer_router",
    )(qp, sk)


def _peer_kernel(eid_cur, eid_nxt, g_ref, hf_ref, x1_ref, gt_ref, gfin_ref, uv_hbm, o_ref, buf, sem, *, nsel):
    nt = pl.num_programs(1)
    step = pl.program_id(0) * nt + pl.program_id(1)
    total = pl.num_programs(0) * nt
    slot = step % 2
    tg = hf_ref.shape[1]
    d = hf_ref.shape[2]

    def row_copy(e, dst_row, s):
        return pltpu.make_async_copy(uv_hbm.at[pl.ds(e, 1)], buf.at[s, pl.ds(dst_row, 1)], sem.at[s])

    def issue(eid_ref, s):
        for tt in range(tg):
            def body(k, c, tt=tt):
                row_copy(eid_ref[tt, k], tt * nsel + k, s).start()
                return c
            lax.fori_loop(0, nsel, body, 0, unroll=8)

    @pl.when(step == 0)
    def _():
        issue(eid_cur, 0)

    @pl.when(step + 1 < total)
    def _():
        issue(eid_nxt, 1 - slot)

    pltpu.make_async_copy(uv_hbm.at[pl.ds(0, tg * nsel)], buf.at[slot], sem.at[slot]).wait()

    eye = lax.broadcasted_iota(I32, (nsel, nsel), 0) == lax.broadcasted_iota(I32, (nsel, nsel), 1)
    ys = []
    for tt in range(tg):
        rows = buf[slot, tt * nsel:(tt + 1) * nsel, :]
        u, v = rows[:, :d], rows[:, d:]
        a = jnp.sum(u * hf_ref[0, tt:tt + 1, :], axis=1, keepdims=True)
        act = 0.5 * a * (1.0 + lax.erf(a * (1.0 / math.sqrt(2.0))))
        grow = jnp.broadcast_to(g_ref[0, tt:tt + 1, :], (nsel, nsel))
        gcol = jnp.sum(jnp.where(eye, grow, 0.0), axis=1, keepdims=True)
        ys.append(jnp.sum((gcol * act) * v, axis=0, keepdims=True))
    y = jnp.concatenate(ys, axis=0)
    x2 = x1_ref[0] + gt_ref[0] * y
    o_ref[0] = _rms(x2, gfin_ref[...])


def _peer(eid, gw, hf, x1, gt, gfin, uv):
    g, tg, d = hf.shape
    n, nsel = eid.shape
    tp = PEER_TOK
    nt = tg // tp
    last = g * nt - 1
    gw3 = gw.reshape(g, tg, nsel)
    return pl.pallas_call(
        functools.partial(_peer_kernel, nsel=nsel),
        grid=(g, nt),
        in_specs=[pl.BlockSpec((tp, nsel), lambda gi, t: (gi * nt + t, 0), memory_space=pltpu.SMEM),
                  pl.BlockSpec((tp, nsel), lambda gi, t: (jnp.minimum(gi * nt + t + 1, last), 0),
                               memory_space=pltpu.SMEM),
                  _xspec(tp, nsel), _xspec(tp, d), _xspec(tp, d), _modspec(gt, tp), _wspec(gfin.shape),
                  pl.BlockSpec(memory_space=pl.ANY)],
        out_specs=_xspec(tp, d),
        out_shape=jax.ShapeDtypeStruct((g, tg, d), F32),
        scratch_shapes=[pltpu.VMEM((2, tp * nsel, uv.shape[1]), uv.dtype), pltpu.SemaphoreType.DMA((2,))],
        compiler_params=_cparams(("arbitrary", "arbitrary")),
        name="peer_experts",
    )(eid, eid, gw3, hf, x1, gt, gfin, uv)


def _rope_tables(pos, half, lane_off, period):
    inv = jnp.exp(-math.log(ROPE_THETA) * jnp.arange(half, dtype=F32) / half)
    ang = pos.astype(F32)[:, None] * inv[None, :]
    cos, sin = jnp.cos(ang), jnp.sin(ang)
    t = pos.shape[0]
    ones = lambda n: jnp.ones((t, n), F32)
    zeros = lambda n: jnp.zeros((t, n), F32)
    tail = period - lane_off - 2 * half
    c = jnp.concatenate([ones(lane_off), cos, cos, ones(tail)], axis=1)
    a = jnp.concatenate([zeros(lane_off), -sin, zeros(half + tail)], axis=1)
    b = jnp.concatenate([zeros(lane_off + half), sin, zeros(tail)], axis=1)
    rep = LANES // period
    return jnp.stack([jnp.tile(c, (1, rep)), jnp.tile(a, (1, rep)), jnp.tile(b, (1, rep))])


def _prep_weights(w_in, w_uq, w_uk, w_uv):
    d = w_in.shape[0]
    sizes = (MLA_Q_LORA, MLA_KV_LORA, MLA_ROPE, MOBA_WIDTH, MOBA_WIDTH, MOBA_WIDTH, D_MODEL, D_MODEL)
    offs = [0]
    for s in sizes:
        offs.append(offs[-1] + s)
    seg = [w_in[:, offs[i]:offs[i + 1]] for i in range(len(sizes))]
    kr = jnp.zeros((d, LANES), w_in.dtype).at[:, KR_LANE:KR_LANE + MLA_ROPE].set(seg[2])
    w_in_p = jnp.concatenate([seg[0], seg[1], kr] + seg[3:], axis=1).astype(BF16)
    hd = MLA_NOPE + MLA_ROPE
    w_uq_p = jnp.pad(w_uq, ((0, 0), (0, 0), (0, LANES - hd))).reshape(w_uq.shape[0], MLA_HEADS * LANES).astype(BF16)
    wuk_p = jnp.pad(w_uk.transpose(1, 0, 2), ((0, 0), (0, 0), (0, LANES - MLA_NOPE)))
    wuk_p = wuk_p.reshape(MLA_KV_LORA, MLA_HEADS * LANES).astype(BF16)
    wukt_p = jnp.pad(w_uk.transpose(0, 2, 1), ((0, 0), (0, LANES - MLA_NOPE), (0, 0))).astype(BF16)
    wuv_all = w_uv.transpose(1, 0, 2).reshape(MLA_KV_LORA, MLA_HEADS * MLA_V).astype(BF16)
    return w_in_p, w_uq_p, wuk_p, wukt_p, wuv_all


def _pad_rows(x, n):
    return jnp.pad(x, ((0, 0), (0, n - x.shape[1])) + ((0, 0),) * (x.ndim - 2))


def kernel(x_prompt, x_sample, cache_mla_latent, cache_mla_kpe, cache_moba_k, cache_moba_v, page_table,
           c_prompt, c_sample, w_ada, b_ada, g_norm_mix, g_norm_ffn, w_in, g_q_lora, w_uq, g_kv_lora, w_uk, w_uv,
           w_o_mla, w_o_moba, w_out, w_peer_q, peer_sub_keys, peer_u, peer_v, g_norm_final):
    assert w_ada.shape[0] == 1, "one trunk layer"
    bp, tp, d = x_prompt.shape
    bs, ts, _ = x_sample.shape
    n_pool = cache_mla_latent.shape[1]
    n_pages = page_table.shape[1]
    past_len = n_pages * PAGE_SIZE
    assert past_len % MOBA_BLOCK == 0, "the sample group's own MoBA block holds only the new tokens"
    nblk_s = past_len // MOBA_BLOCK
    ns = bs * ts

    w_in_p, w_uq_p, wuk_p, wukt_p, wuv_all = _prep_weights(w_in[0], w_uq[0], w_uk[0], w_uv[0])
    w_ada_b = w_ada[0].astype(BF16)
    wom, wob, wout, wpq = (w[0].astype(BF16) for w in (w_o_mla, w_o_moba, w_out, w_peer_q))
    sk = peer_sub_keys[0].astype(BF16)
    uv = jnp.concatenate([peer_u[0], peer_v[0]], axis=1)
    gm, gf, gq, gkv = g_norm_mix[0][None], g_norm_ffn[0][None], g_q_lora[0][None], g_kv_lora[0][None]
    gfin = g_norm_final[None]

    mod = _adaln(jnp.concatenate([c_prompt, c_sample], axis=0), w_ada_b, b_ada[0][None])
    mods_p = [m[:, None, :] for m in jnp.split(mod[:bp], 6, axis=-1)]
    mods_s = [jnp.repeat(m, ts, axis=0)[None] for m in jnp.split(mod[bp:], 6, axis=-1)]

    pos_p = jnp.arange(tp, dtype=I32)
    pos_s = jnp.tile(past_len + jnp.arange(ts, dtype=I32), bs)

    def trunk(x, mods, pos, attend):
        sh_a, sc_a, gt_a, sh_f, sc_f, gt_f = mods
        g, tg, _ = x.shape
        tq = _rope_tables(pos, MLA_ROPE // 2, KR_LANE, LANES)
        tm = _rope_tables(pos, ROT_DIMS // 2, 0, MOBA_HD)
        qn, kvlat, kpe, qmb, kmb, vmb, sg, kmean = _inproj(x, sh_a, sc_a, gm, w_in_p, gq, gkv, tq, tm)
        q = _qproj(qn, w_uq_p, tq)
        o_mla, o_mb = attend(q, kvlat, kpe, qmb, kmb, vmb, kmean)
        x1, hf, qp = _post(o_mla, o_mb, sg, x, gt_a, sc_f, sh_f, wom, wob, wout, gf, wpq)
        eid, gw = _router(qp.reshape(g * tg, -1), sk)
        y = _peer(eid, gw, hf, x1, gt_f, gfin, uv)
        return y, (kvlat, kpe[:, :, KR_LANE:KR_LANE + MLA_ROPE], kmb, vmb)

    def attend_prompt(q, kvlat, kpe, qmb, kmb, vmb, kmean):
        k, v = _kvprep(kvlat, kpe, wuk_p, wuv_all)
        o_mla = _mla_prompt(q, k, v)
        km = _pad_rows(kmean.reshape(kmean.shape[0], kmean.shape[1], MOBA_WIDTH), LANES)
        o_mb = _moba_prompt(qmb, kmb, vmb, km)
        return o_mla, o_mb

    lat_pool = cache_mla_latent.reshape(n_pool, PAGE_SIZE, MLA_KV_LORA)
    kpe_pool = cache_mla_kpe.reshape(n_pool, PAGE_SIZE, MLA_ROPE)
    k_pool = cache_moba_k.reshape(n_pool, PAGE_SIZE, MOBA_HEADS, MOBA_HD)
    v_pool = cache_moba_v.reshape(n_pool, PAGE_SIZE, MOBA_HEADS, MOBA_HD)

    def attend_sample(q, kvlat, kpe, qmb, kmb, vmb, kmean):
        del kmean
        def head_major(a, width):
            return a.reshape(bs, ts, MLA_HEADS, width).transpose(0, 2, 1, 3).reshape(bs, MLA_HEADS * ts, width)
        qlat = head_major(_qlat(q, wukt_p), MLA_KV_LORA)
        qrope = head_major(q.reshape(1, ns, MLA_HEADS, LANES)[..., KR_LANE:KR_LANE + MLA_ROPE], MLA_ROPE)
        lat_new = _pad_rows(kvlat.reshape(bs, ts, MLA_KV_LORA), LANES)
        kpe_new = _pad_rows(kpe.reshape(bs, ts, LANES)[..., KR_LANE:KR_LANE + MLA_ROPE], LANES)
        o_mla = _mla_sample(page_table, qlat, qrope, lat_pool, kpe_pool, lat_new, kpe_new, wuv_all)
        kmean_s = _blockmean(page_table, k_pool, nblk_s)
        qm = head_major(qmb, MOBA_HD)
        k_new = _pad_rows(kmb.reshape(bs, ts * MOBA_HEADS, MOBA_HD), LANES)
        v_new = _pad_rows(vmb.reshape(bs, ts * MOBA_HEADS, MOBA_HD), LANES)
        o = _moba_sample(page_table, qm, kmean_s, k_pool, v_pool, k_new, v_new)
        o_mb = o.reshape(bs, MOBA_HEADS, ts, MOBA_HD).transpose(0, 2, 1, 3).reshape(1, ns, MOBA_WIDTH)
        return o_mla.reshape(1, ns, MLA_HEADS * MLA_V), o_mb.astype(BF16)

    y_p, st_p = trunk(x_prompt, mods_p, pos_p, attend_prompt)
    y_s, st_s = trunk(x_sample.reshape(1, ns, d), mods_s, pos_s, attend_sample)

    def states(st, b, t):
        lat, kpe, kmb, vmb = st
        return (lat.reshape(1, b, t, MLA_KV_LORA), kpe.reshape(1, b, t, MLA_ROPE),
                kmb.reshape(1, b, t, MOBA_HEADS, MOBA_HD), vmb.reshape(1, b, t, MOBA_HEADS, MOBA_HD))

    return (y_p, y_s.reshape(bs, ts, d)) + states(st_p, bp, tp) + states(st_s, bs, ts)
```

```python
import functools
import math

import jax
import jax.numpy as jnp
from jax import lax
from jax.experimental import pallas as pl
from jax.experimental.pallas import tpu as pltpu

F32 = jnp.float32
BF16 = jnp.bfloat16
I32 = jnp.int32

D_MODEL = 1024
PAGE_SIZE = 128
MLA_HEADS = 8
MLA_NOPE = 64
MLA_ROPE = 32
MLA_V = 64
MLA_Q_LORA = 384
MLA_KV_LORA = 256
MLA_SCALE = (MLA_NOPE + MLA_ROPE) ** -0.5
MOBA_HEADS = 8
MOBA_HD = 64
MOBA_WIDTH = MOBA_HEADS * MOBA_HD
MOBA_BLOCK = 256
MOBA_TOPK = 3
MOBA_SCALE = MOBA_HD ** -0.5
PAGES_PER_BLOCK = MOBA_BLOCK // PAGE_SIZE
ROT_DIMS = MOBA_HD // 4
ROPE_THETA = 500000.0
PEER_HEADS = 8
PEER_NKEYS = 128
PEER_DKEY = 256
PEER_TOPK = 16
EPS = 1e-6

LANES = 128
NEG = -1e30
NT_DIMS = (((1,), (1,)), ((), ()))

OFF_QD = 0
OFF_KV = OFF_QD + MLA_Q_LORA
OFF_KR = OFF_KV + MLA_KV_LORA
OFF_QM = OFF_KR + LANES
OFF_KM = OFF_QM + MOBA_WIDTH
OFF_VM = OFF_KM + MOBA_WIDTH
OFF_GT = OFF_VM + MOBA_WIDTH
IN_COLS_P = OFF_GT + 2 * D_MODEL
KR_LANE = MLA_NOPE

TOK_TILE = 256
ATT_TILE = 512
PEER_TOK = 8
VMEM_LIMIT = 56 * 1024 * 1024


def _cparams(sem):
    return pltpu.CompilerParams(dimension_semantics=sem, vmem_limit_bytes=VMEM_LIMIT)


def _rms(x, g):
    return x * lax.rsqrt(jnp.mean(x * x, axis=-1, keepdims=True) + EPS) * g


def _rope_lanes(z, t_ref, half):
    c, a, b = t_ref[0], t_ref[1], t_ref[2]
    outs = []
    for j in range(z.shape[1] // LANES):
        zj = z[:, j * LANES:(j + 1) * LANES]
        outs.append(zj * c + pltpu.roll(zj, LANES - half, 1) * a + pltpu.roll(zj, half, 1) * b)
    return outs[0] if len(outs) == 1 else jnp.concatenate(outs, axis=1)


def _online_update_multi(ss, vs, m_ref, l_ref, acc_ref, pv):
    m_prev = m_ref[...]
    m_new = m_prev
    for s in ss:
        m_new = jnp.maximum(m_new, jnp.max(s, axis=1, keepdims=True))
    alpha = jnp.exp(m_prev - m_new)
    l_new = alpha * l_ref[...]
    a_new = alpha * acc_ref[...]
    for s, v in zip(ss, vs):
        p = jnp.exp(s - m_new)
        l_new = l_new + jnp.sum(p, axis=1, keepdims=True)
        a_new = a_new + pv(p.astype(BF16), v)
    m_ref[...], l_ref[...], acc_ref[...] = m_new, l_new, a_new


def _online_update(s, m_ref, l_ref, acc_ref, v, idx=None):
    m_prev = m_ref[...] if idx is None else m_ref[idx]
    l_prev = l_ref[...] if idx is None else l_ref[idx]
    a_prev = acc_ref[...] if idx is None else acc_ref[idx]
    m_new = jnp.maximum(m_prev, jnp.max(s, axis=1, keepdims=True))
    alpha = jnp.exp(m_prev - m_new)
    p = jnp.exp(s - m_new)
    l_new = alpha * l_prev + jnp.sum(p, axis=1, keepdims=True)
    a_new = alpha * a_prev + jnp.dot(p.astype(BF16), v, preferred_element_type=F32)
    if idx is None:
        m_ref[...], l_ref[...], acc_ref[...] = m_new, l_new, a_new
    else:
        m_ref[idx], l_ref[idx], acc_ref[idx] = m_new, l_new, a_new


def _adaln_kernel(c_ref, w_ref, b_ref, o_ref):
    c = c_ref[...]
    s = c * jax.nn.sigmoid(c)
    o_ref[...] = jnp.dot(s.astype(BF16), w_ref[...], preferred_element_type=F32) + b_ref[...]


def _adaln(c, w, b):
    nb, d = c.shape
    n = w.shape[1]
    tn = n // 4
    return pl.pallas_call(
        _adaln_kernel,
        grid=(n // tn,),
        in_specs=[pl.BlockSpec((nb, d), lambda j: (0, 0)),
                  pl.BlockSpec((d, tn), lambda j: (0, j)),
                  pl.BlockSpec((1, tn), lambda j: (0, j))],
        out_specs=pl.BlockSpec((nb, tn), lambda j: (0, j)),
        out_shape=jax.ShapeDtypeStruct((nb, n), F32),
        compiler_params=_cparams(("arbitrary",)),
        name="adaln",
    )(c, w, b)


def _xspec(tt, w):
    return pl.BlockSpec((1, tt, w), lambda g, t: (g, t, 0))


def _modspec(mod, tt):
    if mod.shape[1] == 1:
        return pl.BlockSpec((1, 1, mod.shape[2]), lambda g, t: (g, 0, 0))
    return pl.BlockSpec((1, tt, mod.shape[2]), lambda g, t: (g, t, 0))


def _wspec(shape):
    nd = len(shape)
    return pl.BlockSpec(shape, lambda g, t: (0,) * nd)


def _tabspec(tt):
    return pl.BlockSpec((3, tt, LANES), lambda g, t: (0, t, 0))


def _inproj_kernel(x_ref, sh_ref, sc_ref, gm_ref, w_ref, gq_ref, gkv_ref, tq_ref, tm_ref,
                   qn_ref, kv_ref, kpe_ref, qmb_ref, kmb_ref, vmb_ref, sg_ref, km_ref):
    x = x_ref[0]
    h = _rms(x, gm_ref[...]) * (1.0 + sc_ref[0]) + sh_ref[0]
    hb = h.astype(BF16)

    def seg(off, n):
        return jnp.dot(hb, w_ref[:, off:off + n], preferred_element_type=F32)

    qn_ref[0] = _rms(seg(OFF_QD, MLA_Q_LORA), gq_ref[...]).astype(BF16)
    kv_ref[0] = _rms(seg(OFF_KV, MLA_KV_LORA), gkv_ref[...])
    kpe_ref[0] = _rope_lanes(seg(OFF_KR, LANES), tq_ref, MLA_ROPE // 2)
    qmb_ref[0] = _rope_lanes(seg(OFF_QM, MOBA_WIDTH), tm_ref, ROT_DIMS // 2).astype(BF16)
    kmb = _rope_lanes(seg(OFF_KM, MOBA_WIDTH), tm_ref, ROT_DIMS // 2)
    kmb_ref[0] = kmb
    km_ref[0, 0] = jnp.mean(kmb, axis=0, keepdims=True)
    vmb_ref[0] = seg(OFF_VM, MOBA_WIDTH)
    sg_ref[0] = jax.nn.sigmoid(seg(OFF_GT, 2 * D_MODEL)).astype(BF16)


def _inproj(x, sh, sc, gm, w, gq, gkv, tq, tm):
    g, tg, d = x.shape
    tt = min(TOK_TILE, tg)
    nt = tg // tt
    outs = [(MLA_Q_LORA, BF16), (MLA_KV_LORA, F32), (LANES, F32), (MOBA_WIDTH, BF16),
            (MOBA_WIDTH, F32), (MOBA_WIDTH, F32), (2 * D_MODEL, BF16)]
    return pl.pallas_call(
        _inproj_kernel,
        grid=(g, nt),
        in_specs=[_xspec(tt, d), _modspec(sh, tt), _modspec(sc, tt), _wspec(gm.shape), _wspec(w.shape),
                  _wspec(gq.shape), _wspec(gkv.shape), _tabspec(tt), _tabspec(tt)],
        out_specs=[_xspec(tt, n) for n, _ in outs]
        + [pl.BlockSpec((1, 1, 1, MOBA_WIDTH), lambda gi, t: (gi, t, 0, 0))],
        out_shape=[jax.ShapeDtypeStruct((g, tg, n), dt) for n, dt in outs]
        + [jax.ShapeDtypeStruct((g, nt, 1, MOBA_WIDTH), F32)],
        compiler_params=_cparams(("parallel", "arbitrary")),
        name="inproj",
    )(x, sh, sc, gm, w, gq, gkv, tq, tm)


def _qproj_kernel(qn_ref, w_ref, tq_ref, q_ref):
    q = jnp.dot(qn_ref[0], w_ref[...], preferred_element_type=F32)
    q_ref[0] = _rope_lanes(q, tq_ref, MLA_ROPE // 2).astype(BF16)


def _qproj(qn, w, tq):
    g, tg, r = qn.shape
    tt = min(TOK_TILE, tg)
    n = w.shape[1]
    return pl.pallas_call(
        _qproj_kernel,
        grid=(g, tg // tt),
        in_specs=[_xspec(tt, r), _wspec(w.shape), _tabspec(tt)],
        out_specs=_xspec(tt, n),
        out_shape=jax.ShapeDtypeStruct((g, tg, n), BF16),
        compiler_params=_cparams(("parallel", "arbitrary")),
        name="mla_qproj",
    )(qn, w, tq)


def _qlat_kernel(q_ref, w_ref, o_ref):
    for h in range(MLA_HEADS):
        qh = q_ref[0, :, h * LANES:(h + 1) * LANES]
        o_ref[0, :, h * MLA_KV_LORA:(h + 1) * MLA_KV_LORA] = jnp.dot(
            qh, w_ref[h], preferred_element_type=F32).astype(BF16)


def _qlat(q, wukt):
    g, tg, n = q.shape
    tt = min(TOK_TILE, tg)
    no = MLA_HEADS * MLA_KV_LORA
    return pl.pallas_call(
        _qlat_kernel,
        grid=(g, tg // tt),
        in_specs=[_xspec(tt, n), _wspec(wukt.shape)],
        out_specs=_xspec(tt, no),
        out_shape=jax.ShapeDtypeStruct((g, tg, no), BF16),
        compiler_params=_cparams(("parallel", "arbitrary")),
        name="mla_qlat",
    )(q, wukt)


def _kvprep_kernel(kv_ref, kpe_ref, wuk_ref, wuv_ref, k_ref, v_ref):
    lat = kv_ref[0].astype(BF16)
    kn = jnp.dot(lat, wuk_ref[...], preferred_element_type=F32)
    kpe = kpe_ref[0]
    k_ref[0] = (kn + jnp.concatenate([kpe] * MLA_HEADS, axis=1)).astype(BF16)
    v_ref[0] = jnp.dot(lat, wuv_ref[...], preferred_element_type=F32).astype(BF16)


def _kvprep(kvlat, kpe, wuk, wuv):
    g, tg, c = kvlat.shape
    tt = min(TOK_TILE, tg)
    return pl.pallas_call(
        _kvprep_kernel,
        grid=(g, tg // tt),
        in_specs=[_xspec(tt, c), _xspec(tt, LANES), _wspec(wuk.shape), _wspec(wuv.shape)],
        out_specs=[_xspec(tt, wuk.shape[1]), _xspec(tt, wuv.shape[1])],
        out_shape=[jax.ShapeDtypeStruct((g, tg, wuk.shape[1]), BF16),
                   jax.ShapeDtypeStruct((g, tg, wuv.shape[1]), BF16)],
        compiler_params=_cparams(("parallel", "arbitrary")),
        name="mla_kvprep",
    )(kvlat, kpe, wuk, wuv)


def _mla_prompt_kernel(q_ref, k_ref, v_ref, o_ref, m_sc, l_sc, acc_sc):
    qi, ki = pl.program_id(1), pl.program_id(2)
    tq = q_ref.shape[1]

    @pl.when(ki == 0)
    def _():
        m_sc[...] = jnp.full(m_sc.shape, NEG, F32)
        l_sc[...] = jnp.zeros(l_sc.shape, F32)
        acc_sc[...] = jnp.zeros(acc_sc.shape, F32)

    @pl.when(ki <= qi)
    def _():
        row = lax.broadcasted_iota(I32, (tq, tq), 0)
        col = lax.broadcasted_iota(I32, (tq, tq), 1)
        keep = jnp.logical_or(col <= row, ki < qi)
        for h in range(MLA_HEADS):
            q = q_ref[0, :, h * LANES:(h + 1) * LANES]
            k = k_ref[0, :, h * LANES:(h + 1) * LANES]
            s = lax.dot_general(q, k, NT_DIMS, preferred_element_type=F32) * MLA_SCALE
            s = jnp.where(keep, s, NEG)
            vp = v_ref[0, :, (h // 2) * LANES:(h // 2 + 1) * LANES]
            _online_update(s, m_sc, l_sc, acc_sc, vp, idx=h)

    @pl.when(ki == qi)
    def _():
        outs = []
        for h in range(MLA_HEADS):
            o = acc_sc[h] / l_sc[h]
            outs.append(o[:, (h % 2) * MLA_V:(h % 2 + 1) * MLA_V])
        o_ref[0] = jnp.concatenate(outs, axis=1).astype(o_ref.dtype)


def _mla_prompt(q, k, v):
    b, t, _ = q.shape
    ta = min(ATT_TILE, t)
    nq = t // ta
    return pl.pallas_call(
        _mla_prompt_kernel,
        grid=(b, nq, nq),
        in_specs=[pl.BlockSpec((1, ta, q.shape[2]), lambda bi, qi, ki: (bi, qi, 0)),
                  pl.BlockSpec((1, ta, k.shape[2]), lambda bi, qi, ki: (bi, jnp.minimum(ki, qi), 0)),
                  pl.BlockSpec((1, ta, v.shape[2]), lambda bi, qi, ki: (bi, jnp.minimum(ki, qi), 0))],
        out_specs=pl.BlockSpec((1, ta, v.shape[2]), lambda bi, qi, ki: (bi, qi, 0)),
        out_shape=jax.ShapeDtypeStruct((b, t, v.shape[2]), BF16),
        scratch_shapes=[pltpu.VMEM((MLA_HEADS, ta, 1), F32), pltpu.VMEM((MLA_HEADS, ta, 1), F32),
                        pltpu.VMEM((MLA_HEADS, ta, LANES), F32)],
        compiler_params=_cparams(("parallel", "arbitrary", "arbitrary")),
        name="mla_prompt_attn",
    )(q, k, v)


def _moba_prompt_kernel(q_ref, k_ref, v_ref, km_ref, o_ref, bias_sc, m_sc, l_sc, acc_sc, *, nblk):
    qi, ki = pl.program_id(1), pl.program_id(2)
    tq = q_ref.shape[1]
    lane = lax.broadcasted_iota(I32, (tq, LANES), 1)

    def head_q(h):
        qp = q_ref[0, :, (h // 2) * LANES:(h // 2 + 1) * LANES]
        return jnp.where((lane // MOBA_HD) == (h % 2), qp, jnp.zeros_like(qp))

    @pl.when(ki == 0)
    def _():
        m_sc[...] = jnp.full(m_sc.shape, NEG, F32)
        l_sc[...] = jnp.zeros(l_sc.shape, F32)
        acc_sc[...] = jnp.zeros(acc_sc.shape, F32)
        for h in range(MOBA_HEADS):
            kmp = km_ref[0, :, (h // 2) * LANES:(h // 2 + 1) * LANES]
            gate = lax.dot_general(head_q(h).astype(F32), kmp, NT_DIMS, preferred_element_type=F32,
                                   precision=lax.Precision.HIGHEST)
            gate = jnp.where(lane < qi, gate, -jnp.inf)
            rank = jnp.zeros((tq, LANES), I32)
            for i in range(nblk):
                c = gate[:, i:i + 1]
                beats = jnp.logical_or(c > gate, jnp.logical_and(c == gate, lane > i))
                rank = rank + beats.astype(I32)
            sel = jnp.logical_and(rank < MOBA_TOPK, lane < qi)
            bias_sc[h] = jnp.where(sel, 0.0, NEG)

    def attend(mask_fn):
        for h in range(MOBA_HEADS):
            kp = k_ref[0, :, (h // 2) * LANES:(h // 2 + 1) * LANES].astype(BF16)
            s = lax.dot_general(head_q(h), kp, NT_DIMS, preferred_element_type=F32) * MOBA_SCALE
            vp = v_ref[0, :, (h // 2) * LANES:(h // 2 + 1) * LANES].astype(BF16)
            _online_update(mask_fn(s, h), m_sc, l_sc, acc_sc, vp, idx=h)

    @pl.when(ki < qi)
    def _():
        def past(s, h):
            colb = jnp.sum(jnp.where(lane == ki, bias_sc[h], 0.0), axis=1, keepdims=True)
            return s + colb
        attend(past)

    @pl.when(ki == qi)
    def _():
        row = lax.broadcasted_iota(I32, (tq, tq), 0)
        col = lax.broadcasted_iota(I32, (tq, tq), 1)
        attend(lambda s, h: jnp.where(col <= row, s, NEG))
        outs = []
        for h in range(MOBA_HEADS):
            o = acc_sc[h] / l_sc[h]
            outs.append(o[:, (h % 2) * MOBA_HD:(h % 2 + 1) * MOBA_HD])
        o_ref[0] = jnp.concatenate(outs, axis=1).astype(o_ref.dtype)


def _moba_prompt(q, k, v, kmean):
    b, t, w = q.shape
    assert t % MOBA_BLOCK == 0
    nblk = t // MOBA_BLOCK
    tb = MOBA_BLOCK
    return pl.pallas_call(
        functools.partial(_moba_prompt_kernel, nblk=nblk),
        grid=(b, nblk, nblk),
        in_specs=[pl.BlockSpec((1, tb, w), lambda bi, qi, ki: (bi, qi, 0)),
                  pl.BlockSpec((1, tb, w), lambda bi, qi, ki: (bi, jnp.minimum(ki, qi), 0)),
                  pl.BlockSpec((1, tb, w), lambda bi, qi, ki: (bi, jnp.minimum(ki, qi), 0)),
                  pl.BlockSpec((1, LANES, w), lambda bi, qi, ki: (bi, 0, 0))],
        out_specs=pl.BlockSpec((1, tb, w), lambda bi, qi, ki: (bi, qi, 0)),
        out_shape=jax.ShapeDtypeStruct((b, t, w), BF16),
        scratch_shapes=[pltpu.VMEM((MOBA_HEADS, tb, LANES), F32), pltpu.VMEM((MOBA_HEADS, tb, 1), F32),
                        pltpu.VMEM((MOBA_HEADS, tb, 1), F32), pltpu.VMEM((MOBA_HEADS, tb, LANES), F32)],
        compiler_params=_cparams(("parallel", "arbitrary", "arbitrary")),
        name="moba_prompt_attn",
    )(q, k, v, kmean)


MLA_PAGES_PER_STEP = 8


def _mla_sample_kernel(pt_ref, ql_ref, qr_ref, *rest):
    npp = MLA_PAGES_PER_STEP
    lat_refs, kpe_refs = rest[:npp], rest[npp:2 * npp]
    latn_ref, kpen_ref, wuv_ref, o_ref, m_sc, l_sc, acc_sc = rest[2 * npp:]
    step = pl.program_id(1)
    rows = ql_ref.shape[1]
    nq = rows // MLA_HEADS

    @pl.when(step == 0)
    def _():
        m_sc[...] = jnp.full(m_sc.shape, NEG, F32)
        l_sc[...] = jnp.zeros(l_sc.shape, F32)
        acc_sc[...] = jnp.zeros(acc_sc.shape, F32)

    ql, qr = ql_ref[0], qr_ref[0]

    def scores(lat, kpe_t):
        s = lax.dot_general(ql, lat, NT_DIMS, preferred_element_type=F32)
        s = s + jnp.dot(qr, kpe_t, preferred_element_type=F32)
        return s * MLA_SCALE

    def pv(p, lat):
        return jnp.dot(p, lat, preferred_element_type=F32)

    lats = [lat_refs[j][0].astype(BF16) for j in range(npp)]
    ss = [scores(lats[j], kpe_refs[j][0].astype(BF16)) for j in range(npp)]
    _online_update_multi(ss, lats, m_sc, l_sc, acc_sc, pv)

    @pl.when(step == pl.num_programs(1) - 1)
    def _():
        latn = latn_ref[0].astype(BF16)
        s = scores(latn, kpen_ref[0].astype(BF16))
        row = lax.broadcasted_iota(I32, s.shape, 0)
        col = lax.broadcasted_iota(I32, s.shape, 1)
        s = jnp.where(col <= (row % nq), s, NEG)
        _online_update_multi([s], [latn], m_sc, l_sc, acc_sc, pv)
        on = (acc_sc[...] / l_sc[...]).astype(BF16)
        t = jnp.dot(on, wuv_ref[...], preferred_element_type=F32)
        r2 = lax.broadcasted_iota(I32, t.shape, 0)
        c2 = lax.broadcasted_iota(I32, t.shape, 1)
        t = jnp.where((c2 // MLA_V) == (r2 // nq), t, 0.0)
        out = t[0:nq]
        for h in range(1, MLA_HEADS):
            out = out + t[h * nq:(h + 1) * nq]
        o_ref[0] = out.astype(o_ref.dtype)


def _mla_sample(page_table, qlat, qrope, lat_pool, kpe_pool, lat_new, kpe_new, wuv):
    b, rows, c = qlat.shape
    n_pages = page_table.shape[1]
    npp = MLA_PAGES_PER_STEP
    assert n_pages % npp == 0
    nq = rows // MLA_HEADS

    def page_spec(shape, j):
        return pl.BlockSpec((1,) + shape, lambda bi, s, pt: (pt[bi, s * npp + j], 0, 0))

    def bspec(shape):
        return pl.BlockSpec((1,) + shape, lambda bi, s, pt: (bi, 0, 0))

    grid_spec = pltpu.PrefetchScalarGridSpec(
        num_scalar_prefetch=1,
        grid=(b, n_pages // npp),
        in_specs=[bspec((rows, c)), bspec((rows, MLA_ROPE))]
        + [page_spec((PAGE_SIZE, c), j) for j in range(npp)]
        + [page_spec((MLA_ROPE, PAGE_SIZE), j) for j in range(npp)]
        + [bspec(lat_new.shape[1:]), bspec(kpe_new.shape[1:]),
           pl.BlockSpec(wuv.shape, lambda bi, s, pt: (0, 0))],
        out_specs=pl.BlockSpec((1, nq, wuv.shape[1]), lambda bi, s, pt: (bi, 0, 0)),
        scratch_shapes=[pltpu.VMEM((rows, 1), F32), pltpu.VMEM((rows, 1), F32), pltpu.VMEM((rows, c), F32)],
    )
    return pl.pallas_call(
        _mla_sample_kernel,
        grid_spec=grid_spec,
        out_shape=jax.ShapeDtypeStruct((b, nq, wuv.shape[1]), BF16),
        compiler_params=_cparams(("parallel", "arbitrary")),
        name="mla_sample_attn",
    )(page_table, qlat, qrope, *([lat_pool] * npp), *([kpe_pool] * npp), lat_new, kpe_new, wuv)


def _moba_sample_kernel(pt_ref, q_ref, *rest, nblk, nq):
    ppb = PAGES_PER_BLOCK
    k_refs, v_refs = rest[:ppb], rest[ppb:2 * ppb]
    kn_ref, vn_ref, o_ref, gate_sc, m_sc, l_sc, acc_sc = rest[2 * ppb:]
    n = pl.program_id(1)
    q = q_ref[0]
    rows = q.shape[0]
    lane = lax.broadcasted_iota(I32, (rows, LANES), 1)

    def pv(p, vt):
        return lax.dot_general(p, vt, NT_DIMS, preferred_element_type=F32)

    @pl.when(n == 0)
    def _():
        gate_sc[...] = jnp.zeros(gate_sc.shape, F32)
        m_sc[...] = jnp.zeros(m_sc.shape, F32)
        l_sc[...] = jnp.zeros(l_sc.shape, F32)

    ss = []
    gsum = jnp.zeros((rows, 1), F32)
    for j in range(ppb):
        kt = k_refs[j][0].reshape(MOBA_WIDTH, PAGE_SIZE).astype(BF16)
        s = jnp.dot(q, kt, preferred_element_type=F32)
        gsum = gsum + jnp.sum(s, axis=1, keepdims=True)
        ss.append(s * MOBA_SCALE)
    mb = jnp.max(ss[0], axis=1, keepdims=True)
    for s in ss[1:]:
        mb = jnp.maximum(mb, jnp.max(s, axis=1, keepdims=True))
    lb = jnp.zeros((rows, 1), F32)
    ob = jnp.zeros((rows, MOBA_WIDTH), F32)
    for j in range(ppb):
        p = jnp.exp(ss[j] - mb)
        lb = lb + jnp.sum(p, axis=1, keepdims=True)
        ob = ob + pv(p.astype(BF16), v_refs[j][0].reshape(MOBA_WIDTH, PAGE_SIZE).astype(BF16))
    hit = lane == n
    gate_sc[...] = jnp.where(hit, gsum * (1.0 / MOBA_BLOCK), gate_sc[...])
    m_sc[...] = jnp.where(hit, mb, m_sc[...])
    l_sc[...] = jnp.where(hit, lb, l_sc[...])
    acc_sc[n] = ob

    @pl.when(n == nblk - 1)
    def _():
        gate = jnp.where(lane < nblk, gate_sc[...], -jnp.inf)
        chosen = jnp.zeros(gate.shape, jnp.bool_)
        for _ in range(min(MOBA_TOPK, nblk)):
            mx = jnp.max(gate, axis=1, keepdims=True)
            first = jnp.min(jnp.where(gate == mx, lane, LANES), axis=1, keepdims=True)
            sel = lane == first
            chosen = jnp.logical_or(chosen, sel)
            gate = jnp.where(sel, -jnp.inf, gate)
        s_own = jnp.dot(q, kn_ref[0].astype(BF16), preferred_element_type=F32) * MOBA_SCALE
        r = lax.broadcasted_iota(I32, s_own.shape, 0)
        s_own = jnp.where(lane <= (r % nq), s_own, NEG)
        m_own = jnp.max(s_own, axis=1, keepdims=True)
        p_own = jnp.exp(s_own - m_own)
        m_all = jnp.maximum(jnp.max(jnp.where(chosen, m_sc[...], NEG), axis=1, keepdims=True), m_own)
        w = jnp.where(chosen, jnp.exp(jnp.where(chosen, m_sc[...], m_all) - m_all), 0.0)
        w_own = jnp.exp(m_own - m_all)
        l_all = jnp.sum(w * l_sc[...], axis=1, keepdims=True) + w_own * jnp.sum(p_own, axis=1, keepdims=True)
        acc = w_own * pv(p_own.astype(BF16), vn_ref[0].astype(BF16))

        def body(i, acc):
            wi = jnp.sum(jnp.where(lane == i, w, 0.0), axis=1, keepdims=True)
            return acc + wi * acc_sc[i]

        acc = lax.fori_loop(0, nblk, body, acc)
        out = acc / l_all
        r2 = lax.broadcasted_iota(I32, out.shape, 0)
        c2 = lax.broadcasted_iota(I32, out.shape, 1)
        out = jnp.where((c2 // MOBA_HD) == (r2 // nq), out, 0.0)
        o = out[0:nq]
        for h in range(1, MOBA_HEADS):
            o = o + out[h * nq:(h + 1) * nq]
        o_ref[0] = o.astype(o_ref.dtype)


def _moba_sample(page_table, qbd, kt_pool, vt_pool, kt_new, vt_new):
    b, rows, w = qbd.shape
    ppb = PAGES_PER_BLOCK
    nblk = page_table.shape[1] // ppb
    assert nblk <= LANES
    nq = rows // MOBA_HEADS

    def page_spec(j):
        return pl.BlockSpec((1, MOBA_HEADS, MOBA_HD, PAGE_SIZE), lambda bi, n, pt: (pt[bi, n * ppb + j], 0, 0, 0))

    def bspec(shape):
        nd = len(shape)
        return pl.BlockSpec((1,) + shape, lambda bi, n, pt: (bi,) + (0,) * nd)

    grid_spec = pltpu.PrefetchScalarGridSpec(
        num_scalar_prefetch=1,
        grid=(b, nblk),
        in_specs=[bspec((rows, w))]
        + [page_spec(j) for j in range(ppb)] + [page_spec(j) for j in range(ppb)]
        + [bspec(kt_new.shape[1:]), bspec(vt_new.shape[1:])],
        out_specs=bspec((nq, w)),
        scratch_shapes=[pltpu.VMEM((rows, LANES), F32), pltpu.VMEM((rows, LANES), F32),
                        pltpu.VMEM((rows, LANES), F32), pltpu.VMEM((nblk, rows, w), F32)],
    )
    return pl.pallas_call(
        functools.partial(_moba_sample_kernel, nblk=nblk, nq=nq),
        grid_spec=grid_spec,
        out_shape=jax.ShapeDtypeStruct((b, nq, w), BF16),
        compiler_params=_cparams(("parallel", "arbitrary")),
        name="moba_sample_attn",
    )(page_table, qbd, *([kt_pool] * ppb), *([vt_pool] * ppb), kt_new, vt_new)


def _post_kernel(om_ref, ob_ref, sg_ref, x_ref, gt_ref, sc_ref, sh_ref, wom_ref, wob_ref, wout_ref,
                 gf_ref, wpq_ref, x1_ref, hf_ref, qp_ref):
    o_mla = jnp.dot(om_ref[0], wom_ref[...], preferred_element_type=F32)
    o_mb = jnp.dot(ob_ref[0], wob_ref[...], preferred_element_type=F32)
    sg = sg_ref[0].astype(F32)
    merged = sg[:, :D_MODEL] * o_mla + sg[:, D_MODEL:] * o_mb
    y = jnp.dot(merged.astype(BF16), wout_ref[...], preferred_element_type=F32)
    x1 = x_ref[0] + gt_ref[0] * y
    x1_ref[0] = x1
    hf = _rms(x1, gf_ref[...]) * (1.0 + sc_ref[0]) + sh_ref[0]
    hf_ref[0] = hf
    qp_ref[0] = jnp.dot(hf.astype(BF16), wpq_ref[...], preferred_element_type=F32).astype(BF16)


def _post(om, ob, sg, x, gt, sc, sh, wom, wob, wout, gf, wpq):
    g, tg, d = x.shape
    tt = min(TOK_TILE, tg)
    nq = wpq.shape[1]
    return pl.pallas_call(
        _post_kernel,
        grid=(g, tg // tt),
        in_specs=[_xspec(tt, om.shape[2]), _xspec(tt, ob.shape[2]), _xspec(tt, sg.shape[2]), _xspec(tt, d),
                  _modspec(gt, tt), _modspec(sc, tt), _modspec(sh, tt),
                  _wspec(wom.shape), _wspec(wob.shape), _wspec(wout.shape), _wspec(gf.shape), _wspec(wpq.shape)],
        out_specs=[_xspec(tt, d), _xspec(tt, d), _xspec(tt, nq)],
        out_shape=[jax.ShapeDtypeStruct((g, tg, d), F32), jax.ShapeDtypeStruct((g, tg, d), F32),
                   jax.ShapeDtypeStruct((g, tg, nq), BF16)],
        compiler_params=_cparams(("parallel", "arbitrary")),
        name="post_attn",
    )(om, ob, sg, x, gt, sc, sh, wom, wob, wout, gf, wpq)


def _topk_rows(s, k, val_sc, idx_sc, payload=None):
    nrow = s.shape[0]
    rid = lax.broadcasted_iota(I32, s.shape, 0)

    def body(r, s):
        mx = jnp.max(s, axis=0, keepdims=True)
        first = jnp.min(jnp.where(s == mx, rid, nrow), axis=0, keepdims=True)
        hit = rid == first
        val_sc[pl.ds(r, 1), :] = mx
        if payload is None:
            idx_sc[pl.ds(r, 1), :] = first
        else:
            idx_sc[pl.ds(r, 1), :] = jnp.max(jnp.where(hit, payload, -1), axis=0, keepdims=True)
        return jnp.where(hit, -jnp.inf, s)

    lax.fori_loop(0, k, body, s)


def _router_kernel(qp_ref, sk_ref, eid_ref, g_ref, v1_sc, i1_sc, v2_sc, i2_sc, vb_sc, ib_sc, e_sc, w_sc):
    kk = PEER_TOPK
    half = PEER_DKEY // 2
    for h in range(PEER_HEADS):
        tops = []
        for p, (v_sc, i_sc) in enumerate(((v1_sc, i1_sc), (v2_sc, i2_sc))):
            qhp = qp_ref[:, (2 * h + p) * half:(2 * h + p + 1) * half]
            s = lax.dot_general(sk_ref[h, p], qhp, NT_DIMS, preferred_element_type=F32)
            _topk_rows(s, kk, v_sc, i_sc)
            tops.append((v_sc[...], i_sc[...]))
        (a, ia), (b, ib) = tops
        hk = kk // 2
        jr = lax.broadcasted_iota(I32, (hk, a.shape[1]), 0)
        cvs, cis = [], []
        for i in range(hk):
            nj = kk if i == 0 else hk
            cv = a[i:i + 1, :] + b[:nj]
            if kk // (i + 1) < nj:
                cv = jnp.where(jr < kk // (i + 1), cv, -jnp.inf)
            cvs.append(cv)
            cis.append(ia[i:i + 1, :] * PEER_NKEYS + ib[:nj])
        cvs.append(a[hk:] + b[0:1, :])
        cis.append(ia[hk:] * PEER_NKEYS + ib[0:1, :])
        _topk_rows(jnp.concatenate(cvs, axis=0), kk, vb_sc, ib_sc, payload=jnp.concatenate(cis, axis=0))
        best = vb_sc[...]
        ex = jnp.exp(best - jnp.max(best, axis=0, keepdims=True))
        w_sc[h * kk:(h + 1) * kk, :] = ex / jnp.sum(ex, axis=0, keepdims=True)
        e_sc[h * kk:(h + 1) * kk, :] = ib_sc[...]
    eid_ref[...] = e_sc[...].T
    g_ref[...] = w_sc[...].T


def _router(qp, sk):
    n, w = qp.shape
    tt = min(TOK_TILE, n)
    kk = PEER_TOPK
    nsel = PEER_HEADS * kk
    return pl.pallas_call(
        _router_kernel,
        grid=(n // tt,),
        in_specs=[pl.BlockSpec((tt, w), lambda i: (i, 0)),
                  pl.BlockSpec(sk.shape, lambda i: (0, 0, 0, 0))],
        out_specs=[pl.BlockSpec((tt, nsel), lambda i: (i, 0)), pl.BlockSpec((tt, nsel), lambda i: (i, 0))],
        out_shape=[jax.ShapeDtypeStruct((n, nsel), I32), jax.ShapeDtypeStruct((n, nsel), F32)],
        scratch_shapes=[pltpu.VMEM((kk, tt), F32), pltpu.VMEM((kk, tt), I32),
                        pltpu.VMEM((kk, tt), F32), pltpu.VMEM((kk, tt), I32),
                        pltpu.VMEM((kk, tt), F32), pltpu.VMEM((kk, tt), I32),
                        pltpu.VMEM((nsel, tt), I32), pltpu.VMEM((nsel, tt), F32)],
        compiler_params=_cparams(("parallel",)),
        name="peer_router",
    )(qp, sk)


def _peer_kernel(eid_cur, eid_nxt, g_ref, hf_ref, x1_ref, gt_ref, gfin_ref, uv_hbm, o_ref, buf, sem, *, nsel):
    nt = pl.num_programs(1)
    step = pl.program_id(0) * nt + pl.program_id(1)
    total = pl.num_programs(0) * nt
    tg = hf_ref.shape[1]

    def row_copy(e, dst_row, s):
        return pltpu.make_async_copy(uv_hbm.at[pl.ds(e, 1)], buf.at[s, pl.ds(dst_row, 1)], sem.at[s])

    @pl.when(step == 0)
    def _():
        for tt in range(tg):
            def body(k, c, tt=tt):
                row_copy(eid_cur[tt, k], tt * nsel + k, 0).start()
                return c
            lax.fori_loop(0, nsel, body, 0)

    def wait_slot(s):
        pltpu.make_async_copy(uv_hbm.at[pl.ds(0, tg * nsel)], buf.at[s], sem.at[s]).wait()

    def run(slot):
        wait_slot(slot)
        eye = lax.broadcasted_iota(I32, (nsel, nsel), 0) == lax.broadcasted_iota(I32, (nsel, nsel), 1)
        ys = []
        for tt in range(tg):
            for k in range(nsel):
                row_copy(eid_nxt[tt, k], tt * nsel + k, 1 - slot).start(priority=k % 2)
            rows = buf[slot, tt * nsel:(tt + 1) * nsel, :]
            u = lax.bitcast_convert_type(rows & jnp.uint32(0xFFFF0000), F32)
            v = lax.bitcast_convert_type(rows << 16, F32)
            a = jnp.sum(u * hf_ref[0, tt:tt + 1, :], axis=1, keepdims=True)
            act = 0.5 * a * (1.0 + lax.erf(a * (1.0 / math.sqrt(2.0))))
            grow = jnp.broadcast_to(g_ref[0, tt:tt + 1, :], (nsel, nsel))
            gcol = jnp.sum(jnp.where(eye, grow, 0.0), axis=1, keepdims=True)
            ys.append(jnp.sum((gcol * act) * v, axis=0, keepdims=True))
        y = jnp.concatenate(ys, axis=0)
        x2 = x1_ref[0] + gt_ref[0] * y
        o_ref[0] = _rms(x2, gfin_ref[...])

        @pl.when(step + 1 == total)
        def _():
            wait_slot(1 - slot)

    for slot in (0, 1):
        pl.when(step % 2 == slot)(functools.partial(run, slot))


def _peer(eid, gw, hf, x1, gt, gfin, uv):
    g, tg, d = hf.shape
    n, nsel = eid.shape
    tp = PEER_TOK
    nt = tg // tp
    last = g * nt - 1
    gw3 = gw.reshape(g, tg, nsel)
    return pl.pallas_call(
        functools.partial(_peer_kernel, nsel=nsel),
        grid=(g, nt),
        in_specs=[pl.BlockSpec((tp, nsel), lambda gi, t: (gi * nt + t, 0), memory_space=pltpu.SMEM),
                  pl.BlockSpec((tp, nsel), lambda gi, t: (jnp.minimum(gi * nt + t + 1, last), 0),
                               memory_space=pltpu.SMEM),
                  _xspec(tp, nsel), _xspec(tp, d), _xspec(tp, d), _modspec(gt, tp), _wspec(gfin.shape),
                  pl.BlockSpec(memory_space=pl.ANY)],
        out_specs=_xspec(tp, d),
        out_shape=jax.ShapeDtypeStruct((g, tg, d), F32),
        scratch_shapes=[pltpu.VMEM((2, tp * nsel, uv.shape[1]), uv.dtype), pltpu.SemaphoreType.DMA((2,))],
        compiler_params=_cparams(("arbitrary", "arbitrary")),
        name="peer_experts",
    )(eid, eid, gw3, hf, x1, gt, gfin, uv)


def _rope_tables(pos, half, lane_off, period):
    inv = jnp.exp(-math.log(ROPE_THETA) * jnp.arange(half, dtype=F32) / half)
    ang = pos.astype(F32)[:, None] * inv[None, :]
    cos, sin = jnp.cos(ang), jnp.sin(ang)
    t = pos.shape[0]
    ones = lambda n: jnp.ones((t, n), F32)
    zeros = lambda n: jnp.zeros((t, n), F32)
    tail = period - lane_off - 2 * half
    c = jnp.concatenate([ones(lane_off), cos, cos, ones(tail)], axis=1)
    a = jnp.concatenate([zeros(lane_off), -sin, zeros(half + tail)], axis=1)
    b = jnp.concatenate([zeros(lane_off + half), sin, zeros(tail)], axis=1)
    rep = LANES // period
    return jnp.stack([jnp.tile(c, (1, rep)), jnp.tile(a, (1, rep)), jnp.tile(b, (1, rep))])


def _prep_weights(w_in, w_uq, w_uk, w_uv):
    d = w_in.shape[0]
    sizes = (MLA_Q_LORA, MLA_KV_LORA, MLA_ROPE, MOBA_WIDTH, MOBA_WIDTH, MOBA_WIDTH, D_MODEL, D_MODEL)
    offs = [0]
    for s in sizes:
        offs.append(offs[-1] + s)
    seg = [w_in[:, offs[i]:offs[i + 1]] for i in range(len(sizes))]
    kr = jnp.zeros((d, LANES), w_in.dtype).at[:, KR_LANE:KR_LANE + MLA_ROPE].set(seg[2])
    w_in_p = jnp.concatenate([seg[0], seg[1], kr] + seg[3:], axis=1).astype(BF16)
    hd = MLA_NOPE + MLA_ROPE
    w_uq_p = jnp.pad(w_uq, ((0, 0), (0, 0), (0, LANES - hd))).reshape(w_uq.shape[0], MLA_HEADS * LANES).astype(BF16)
    wuk_p = jnp.pad(w_uk.transpose(1, 0, 2), ((0, 0), (0, 0), (0, LANES - MLA_NOPE)))
    wuk_p = wuk_p.reshape(MLA_KV_LORA, MLA_HEADS * LANES).astype(BF16)
    wukt_p = jnp.pad(w_uk.transpose(0, 2, 1), ((0, 0), (0, LANES - MLA_NOPE), (0, 0))).astype(BF16)
    wuv_all = w_uv.transpose(1, 0, 2).reshape(MLA_KV_LORA, MLA_HEADS * MLA_V).astype(BF16)
    return w_in_p, w_uq_p, wuk_p, wukt_p, wuv_all


def _pad_rows(x, n):
    return jnp.pad(x, ((0, 0), (0, n - x.shape[1])) + ((0, 0),) * (x.ndim - 2))


def kernel(x_prompt, x_sample, cache_mla_latent, cache_mla_kpe, cache_moba_k, cache_moba_v, page_table,
           c_prompt, c_sample, w_ada, b_ada, g_norm_mix, g_norm_ffn, w_in, g_q_lora, w_uq, g_kv_lora, w_uk, w_uv,
           w_o_mla, w_o_moba, w_out, w_peer_q, peer_sub_keys, peer_u, peer_v, g_norm_final):
    assert w_ada.shape[0] == 1, "one trunk layer"
    bp, tp, d = x_prompt.shape
    bs, ts, _ = x_sample.shape
    n_pool = cache_mla_latent.shape[1]
    n_pages = page_table.shape[1]
    past_len = n_pages * PAGE_SIZE
    assert past_len % MOBA_BLOCK == 0, "the sample group's own MoBA block holds only the new tokens"
    nblk_s = past_len // MOBA_BLOCK
    ns = bs * ts

    w_in_p, w_uq_p, wuk_p, wukt_p, wuv_all = _prep_weights(w_in[0], w_uq[0], w_uk[0], w_uv[0])
    w_ada_b = w_ada[0].astype(BF16)
    wom, wob, wout, wpq = (w[0].astype(BF16) for w in (w_o_mla, w_o_moba, w_out, w_peer_q))
    sk = peer_sub_keys[0].astype(BF16)
    def bf16_bits(t):
        return lax.bitcast_convert_type(t.astype(BF16), jnp.uint16).astype(jnp.uint32)
    uv = (bf16_bits(peer_u[0]) << 16) | bf16_bits(peer_v[0])
    gm, gf, gq, gkv = g_norm_mix[0][None], g_norm_ffn[0][None], g_q_lora[0][None], g_kv_lora[0][None]
    gfin = g_norm_final[None]

    mod = _adaln(jnp.concatenate([c_prompt, c_sample], axis=0), w_ada_b, b_ada[0][None])
    mods_p = [m[:, None, :] for m in jnp.split(mod[:bp], 6, axis=-1)]
    mods_s = [jnp.repeat(m, ts, axis=0)[None] for m in jnp.split(mod[bp:], 6, axis=-1)]

    pos_p = jnp.arange(tp, dtype=I32)
    pos_s = jnp.tile(past_len + jnp.arange(ts, dtype=I32), bs)

    def trunk(x, mods, pos, attend):
        sh_a, sc_a, gt_a, sh_f, sc_f, gt_f = mods
        g, tg, _ = x.shape
        tq = _rope_tables(pos, MLA_ROPE // 2, KR_LANE, LANES)
        tm = _rope_tables(pos, ROT_DIMS // 2, 0, MOBA_HD)
        qn, kvlat, kpe, qmb, kmb, vmb, sg, kmean = _inproj(x, sh_a, sc_a, gm, w_in_p, gq, gkv, tq, tm)
        q = _qproj(qn, w_uq_p, tq)
        o_mla, o_mb = attend(q, kvlat, kpe, qmb, kmb, vmb, kmean)
        x1, hf, qp = _post(o_mla, o_mb, sg, x, gt_a, sc_f, sh_f, wom, wob, wout, gf, wpq)
        eid, gw = _router(qp.reshape(g * tg, -1), sk)
        y = _peer(eid, gw, hf, x1, gt_f, gfin, uv)
        return y, (kvlat, kpe[:, :, KR_LANE:KR_LANE + MLA_ROPE], kmb, vmb)

    def attend_prompt(q, kvlat, kpe, qmb, kmb, vmb, kmean):
        k, v = _kvprep(kvlat, kpe, wuk_p, wuv_all)
        o_mla = _mla_prompt(q, k, v)
        km = _pad_rows(kmean.reshape(kmean.shape[0], kmean.shape[1], MOBA_WIDTH), LANES)
        o_mb = _moba_prompt(qmb, kmb, vmb, km)
        return o_mla, o_mb

    lat_pool = cache_mla_latent.reshape(n_pool, PAGE_SIZE, MLA_KV_LORA)
    kpe_pool = jnp.transpose(cache_mla_kpe[0], (0, 2, 1))
    kt_pool = jnp.transpose(cache_moba_k[0], (0, 2, 3, 1))
    vt_pool = jnp.transpose(cache_moba_v[0], (0, 2, 3, 1))

    def key_minor(a):
        return _pad_rows(a, LANES).transpose(0, 2, 1)

    def attend_sample(q, kvlat, kpe, qmb, kmb, vmb, kmean):
        del kmean
        def head_major(a, width):
            return a.reshape(bs, ts, MLA_HEADS, width).transpose(0, 2, 1, 3).reshape(bs, MLA_HEADS * ts, width)
        qlat = head_major(_qlat(q, wukt_p), MLA_KV_LORA)
        qrope = head_major(q.reshape(1, ns, MLA_HEADS, LANES)[..., KR_LANE:KR_LANE + MLA_ROPE], MLA_ROPE)
        lat_new = _pad_rows(kvlat.reshape(bs, ts, MLA_KV_LORA), LANES)
        kpe_new = key_minor(kpe.reshape(bs, ts, LANES)[..., KR_LANE:KR_LANE + MLA_ROPE])
        o_mla = _mla_sample(page_table, qlat, qrope, lat_pool, kpe_pool, lat_new, kpe_new, wuv_all)
        qm = qmb.reshape(bs, ts, MOBA_HEADS, MOBA_HD).transpose(0, 2, 1, 3)
        eye = jnp.eye(MOBA_HEADS, dtype=qm.dtype)
        qbd = (qm[:, :, :, None, :] * eye[None, :, None, :, None]).reshape(bs, MOBA_HEADS * ts, MOBA_WIDTH)
        o_mb = _moba_sample(page_table, qbd, kt_pool, vt_pool,
                            key_minor(kmb.reshape(bs, ts, MOBA_WIDTH)), key_minor(vmb.reshape(bs, ts, MOBA_WIDTH)))
        return o_mla.reshape(1, ns, MLA_HEADS * MLA_V), o_mb.reshape(1, ns, MOBA_WIDTH)

    y_p, st_p = trunk(x_prompt, mods_p, pos_p, attend_prompt)
    y_s, st_s = trunk(x_sample.reshape(1, ns, d), mods_s, pos_s, attend_sample)

    def states(st, b, t):
        lat, kpe, kmb, vmb = st
        return (lat.reshape(1, b, t, MLA_KV_LORA), kpe.reshape(1, b, t, MLA_ROPE),
                kmb.reshape(1, b, t, MOBA_HEADS, MOBA_HD), vmb.reshape(1, b, t, MOBA_HEADS, MOBA_HD))

    return (y_p, y_s.reshape(bs, ts, d)) + states(st_p, bp, tp) + states(st_s, bs, ts)
```

```python
import functools
import math

import jax
import jax.numpy as jnp
from jax import lax
from jax.experimental import pallas as pl
from jax.experimental.pallas import tpu as pltpu

F32 = jnp.float32
BF16 = jnp.bfloat16
I32 = jnp.int32

D_MODEL = 1024
PAGE_SIZE = 128
MLA_HEADS = 8
MLA_NOPE = 64
MLA_ROPE = 32
MLA_V = 64
MLA_Q_LORA = 384
MLA_KV_LORA = 256
MLA_SCALE = (MLA_NOPE + MLA_ROPE) ** -0.5
MOBA_HEADS = 8
MOBA_HD = 64
MOBA_WIDTH = MOBA_HEADS * MOBA_HD
MOBA_BLOCK = 256
MOBA_TOPK = 3
MOBA_SCALE = MOBA_HD ** -0.5
PAGES_PER_BLOCK = MOBA_BLOCK // PAGE_SIZE
ROT_DIMS = MOBA_HD // 4
ROPE_THETA = 500000.0
PEER_HEADS = 8
PEER_NKEYS = 128
PEER_DKEY = 256
PEER_TOPK = 16
EPS = 1e-6

LANES = 128
NEG = -1e30
NT_DIMS = (((1,), (1,)), ((), ()))

OFF_QD = 0
OFF_KV = OFF_QD + MLA_Q_LORA
OFF_KR = OFF_KV + MLA_KV_LORA
OFF_QM = OFF_KR + LANES
OFF_KM = OFF_QM + MOBA_WIDTH
OFF_VM = OFF_KM + MOBA_WIDTH
OFF_GT = OFF_VM + MOBA_WIDTH
IN_COLS_P = OFF_GT + 2 * D_MODEL
KR_LANE = MLA_NOPE

TOK_TILE = 256
ATT_TILE = 512
PEER_TOK = 8
VMEM_LIMIT = 56 * 1024 * 1024


def _cparams(sem):
    return pltpu.CompilerParams(dimension_semantics=sem, vmem_limit_bytes=VMEM_LIMIT)


def _rms(x, g):
    return x * lax.rsqrt(jnp.mean(x * x, axis=-1, keepdims=True) + EPS) * g


def _rope_lanes(z, t_ref, half):
    c, a, b = t_ref[0], t_ref[1], t_ref[2]
    outs = []
    for j in range(z.shape[1] // LANES):
        zj = z[:, j * LANES:(j + 1) * LANES]
        outs.append(zj * c + pltpu.roll(zj, LANES - half, 1) * a + pltpu.roll(zj, half, 1) * b)
    return outs[0] if len(outs) == 1 else jnp.concatenate(outs, axis=1)


def _online_update_multi(ss, vs, m_ref, l_ref, acc_ref, pv):
    m_prev = m_ref[...]
    m_new = m_prev
    for s in ss:
        m_new = jnp.maximum(m_new, jnp.max(s, axis=1, keepdims=True))
    alpha = jnp.exp(m_prev - m_new)
    l_new = alpha * l_ref[...]
    a_new = alpha * acc_ref[...]
    for s, v in zip(ss, vs):
        p = jnp.exp(s - m_new)
        l_new = l_new + jnp.sum(p, axis=1, keepdims=True)
        a_new = a_new + pv(p.astype(BF16), v)
    m_ref[...], l_ref[...], acc_ref[...] = m_new, l_new, a_new


def _online_update(s, m_ref, l_ref, acc_ref, v, idx=None):
    m_prev = m_ref[...] if idx is None else m_ref[idx]
    l_prev = l_ref[...] if idx is None else l_ref[idx]
    a_prev = acc_ref[...] if idx is None else acc_ref[idx]
    m_new = jnp.maximum(m_prev, jnp.max(s, axis=1, keepdims=True))
    alpha = jnp.exp(m_prev - m_new)
    p = jnp.exp(s - m_new)
    l_new = alpha * l_prev + jnp.sum(p, axis=1, keepdims=True)
    a_new = alpha * a_prev + jnp.dot(p.astype(BF16), v, preferred_element_type=F32)
    if idx is None:
        m_ref[...], l_ref[...], acc_ref[...] = m_new, l_new, a_new
    else:
        m_ref[idx], l_ref[idx], acc_ref[idx] = m_new, l_new, a_new


def _adaln_kernel(c_ref, w_ref, b_ref, o_ref):
    c = c_ref[...]
    s = c * jax.nn.sigmoid(c)
    o_ref[...] = jnp.dot(s.astype(BF16), w_ref[...], preferred_element_type=F32) + b_ref[...]


def _adaln(c, w, b):
    nb, d = c.shape
    n = w.shape[1]
    tn = n // 4
    return pl.pallas_call(
        _adaln_kernel,
        grid=(n // tn,),
        in_specs=[pl.BlockSpec((nb, d), lambda j: (0, 0)),
                  pl.BlockSpec((d, tn), lambda j: (0, j)),
                  pl.BlockSpec((1, tn), lambda j: (0, j))],
        out_specs=pl.BlockSpec((nb, tn), lambda j: (0, j)),
        out_shape=jax.ShapeDtypeStruct((nb, n), F32),
        compiler_params=_cparams(("arbitrary",)),
        name="adaln",
    )(c, w, b)


def _xspec(tt, w):
    return pl.BlockSpec((1, tt, w), lambda g, t: (g, t, 0))


def _modspec(mod, tt):
    if mod.shape[1] == 1:
        return pl.BlockSpec((1, 1, mod.shape[2]), lambda g, t: (g, 0, 0))
    return pl.BlockSpec((1, tt, mod.shape[2]), lambda g, t: (g, t, 0))


def _wspec(shape):
    nd = len(shape)
    return pl.BlockSpec(shape, lambda g, t: (0,) * nd)


def _tabspec(tt):
    return pl.BlockSpec((3, tt, LANES), lambda g, t: (0, t, 0))


def _inproj_kernel(x_ref, sh_ref, sc_ref, gm_ref, w_ref, gq_ref, gkv_ref, tq_ref, tm_ref,
                   qn_ref, kv_ref, kpe_ref, qmb_ref, kmb_ref, vmb_ref, sg_ref, km_ref):
    x = x_ref[0]
    h = _rms(x, gm_ref[...]) * (1.0 + sc_ref[0]) + sh_ref[0]
    hb = h.astype(BF16)

    def seg(off, n):
        return jnp.dot(hb, w_ref[:, off:off + n], preferred_element_type=F32)

    qn_ref[0] = _rms(seg(OFF_QD, MLA_Q_LORA), gq_ref[...]).astype(BF16)
    kv_ref[0] = _rms(seg(OFF_KV, MLA_KV_LORA), gkv_ref[...])
    kpe_ref[0] = _rope_lanes(seg(OFF_KR, LANES), tq_ref, MLA_ROPE // 2)
    qmb_ref[0] = _rope_lanes(seg(OFF_QM, MOBA_WIDTH), tm_ref, ROT_DIMS // 2).astype(BF16)
    kmb = _rope_lanes(seg(OFF_KM, MOBA_WIDTH), tm_ref, ROT_DIMS // 2)
    kmb_ref[0] = kmb
    km_ref[0, 0] = jnp.mean(kmb, axis=0, keepdims=True)
    vmb_ref[0] = seg(OFF_VM, MOBA_WIDTH)
    sg_ref[0] = jax.nn.sigmoid(seg(OFF_GT, 2 * D_MODEL)).astype(BF16)


def _inproj(x, sh, sc, gm, w, gq, gkv, tq, tm):
    g, tg, d = x.shape
    tt = min(TOK_TILE, tg)
    nt = tg // tt
    outs = [(MLA_Q_LORA, BF16), (MLA_KV_LORA, F32), (LANES, F32), (MOBA_WIDTH, BF16),
            (MOBA_WIDTH, F32), (MOBA_WIDTH, F32), (2 * D_MODEL, BF16)]
    return pl.pallas_call(
        _inproj_kernel,
        grid=(g, nt),
        in_specs=[_xspec(tt, d), _modspec(sh, tt), _modspec(sc, tt), _wspec(gm.shape), _wspec(w.shape),
                  _wspec(gq.shape), _wspec(gkv.shape), _tabspec(tt), _tabspec(tt)],
        out_specs=[_xspec(tt, n) for n, _ in outs]
        + [pl.BlockSpec((1, 1, 1, MOBA_WIDTH), lambda gi, t: (gi, t, 0, 0))],
        out_shape=[jax.ShapeDtypeStruct((g, tg, n), dt) for n, dt in outs]
        + [jax.ShapeDtypeStruct((g, nt, 1, MOBA_WIDTH), F32)],
        compiler_params=_cparams(("parallel", "arbitrary")),
        name="inproj",
    )(x, sh, sc, gm, w, gq, gkv, tq, tm)


def _qproj_kernel(qn_ref, w_ref, tq_ref, q_ref):
    q = jnp.dot(qn_ref[0], w_ref[...], preferred_element_type=F32)
    q_ref[0] = _rope_lanes(q, tq_ref, MLA_ROPE // 2).astype(BF16)


def _qproj(qn, w, tq):
    g, tg, r = qn.shape
    tt = min(TOK_TILE, tg)
    n = w.shape[1]
    return pl.pallas_call(
        _qproj_kernel,
        grid=(g, tg // tt),
        in_specs=[_xspec(tt, r), _wspec(w.shape), _tabspec(tt)],
        out_specs=_xspec(tt, n),
        out_shape=jax.ShapeDtypeStruct((g, tg, n), BF16),
        compiler_params=_cparams(("parallel", "arbitrary")),
        name="mla_qproj",
    )(qn, w, tq)


def _qlat_kernel(q_ref, w_ref, o_ref):
    for h in range(MLA_HEADS):
        qh = q_ref[0, :, h * LANES:(h + 1) * LANES]
        o_ref[0, :, h * MLA_KV_LORA:(h + 1) * MLA_KV_LORA] = jnp.dot(
            qh, w_ref[h], preferred_element_type=F32).astype(BF16)


def _qlat(q, wukt):
    g, tg, n = q.shape
    tt = min(TOK_TILE, tg)
    no = MLA_HEADS * MLA_KV_LORA
    return pl.pallas_call(
        _qlat_kernel,
        grid=(g, tg // tt),
        in_specs=[_xspec(tt, n), _wspec(wukt.shape)],
        out_specs=_xspec(tt, no),
        out_shape=jax.ShapeDtypeStruct((g, tg, no), BF16),
        compiler_params=_cparams(("parallel", "arbitrary")),
        name="mla_qlat",
    )(q, wukt)


def _kvprep_kernel(kv_ref, kpe_ref, wuk_ref, wuv_ref, k_ref, v_ref):
    lat = kv_ref[0].astype(BF16)
    kn = jnp.dot(lat, wuk_ref[...], preferred_element_type=F32)
    kpe = kpe_ref[0]
    k_ref[0] = (kn + jnp.concatenate([kpe] * MLA_HEADS, axis=1)).astype(BF16)
    v_ref[0] = jnp.dot(lat, wuv_ref[...], preferred_element_type=F32).astype(BF16)


def _kvprep(kvlat, kpe, wuk, wuv):
    g, tg, c = kvlat.shape
    tt = min(TOK_TILE, tg)
    return pl.pallas_call(
        _kvprep_kernel,
        grid=(g, tg // tt),
        in_specs=[_xspec(tt, c), _xspec(tt, LANES), _wspec(wuk.shape), _wspec(wuv.shape)],
        out_specs=[_xspec(tt, wuk.shape[1]), _xspec(tt, wuv.shape[1])],
        out_shape=[jax.ShapeDtypeStruct((g, tg, wuk.shape[1]), BF16),
                   jax.ShapeDtypeStruct((g, tg, wuv.shape[1]), BF16)],
        compiler_params=_cparams(("parallel", "arbitrary")),
        name="mla_kvprep",
    )(kvlat, kpe, wuk, wuv)


def _mla_prompt_kernel(q_ref, k_ref, v_ref, o_ref, m_sc, l_sc, acc_sc):
    qi, ki = pl.program_id(1), pl.program_id(2)
    tq = q_ref.shape[1]

    @pl.when(ki == 0)
    def _():
        m_sc[...] = jnp.full(m_sc.shape, NEG, F32)
        l_sc[...] = jnp.zeros(l_sc.shape, F32)
        acc_sc[...] = jnp.zeros(acc_sc.shape, F32)

    @pl.when(ki <= qi)
    def _():
        row = lax.broadcasted_iota(I32, (tq, tq), 0)
        col = lax.broadcasted_iota(I32, (tq, tq), 1)
        keep = jnp.logical_or(col <= row, ki < qi)
        for h in range(MLA_HEADS):
            q = q_ref[0, :, h * LANES:(h + 1) * LANES]
            k = k_ref[0, :, h * LANES:(h + 1) * LANES]
            s = lax.dot_general(q, k, NT_DIMS, preferred_element_type=F32) * MLA_SCALE
            s = jnp.where(keep, s, NEG)
            vp = v_ref[0, :, (h // 2) * LANES:(h // 2 + 1) * LANES]
            _online_update(s, m_sc, l_sc, acc_sc, vp, idx=h)

    @pl.when(ki == qi)
    def _():
        outs = []
        for h in range(MLA_HEADS):
            o = acc_sc[h] / l_sc[h]
            outs.append(o[:, (h % 2) * MLA_V:(h % 2 + 1) * MLA_V])
        o_ref[0] = jnp.concatenate(outs, axis=1).astype(o_ref.dtype)


def _mla_prompt(q, k, v):
    b, t, _ = q.shape
    ta = min(ATT_TILE, t)
    nq = t // ta
    return pl.pallas_call(
        _mla_prompt_kernel,
        grid=(b, nq, nq),
        in_specs=[pl.BlockSpec((1, ta, q.shape[2]), lambda bi, qi, ki: (bi, qi, 0)),
                  pl.BlockSpec((1, ta, k.shape[2]), lambda bi, qi, ki: (bi, jnp.minimum(ki, qi), 0)),
                  pl.BlockSpec((1, ta, v.shape[2]), lambda bi, qi, ki: (bi, jnp.minimum(ki, qi), 0))],
        out_specs=pl.BlockSpec((1, ta, v.shape[2]), lambda bi, qi, ki: (bi, qi, 0)),
        out_shape=jax.ShapeDtypeStruct((b, t, v.shape[2]), BF16),
        scratch_shapes=[pltpu.VMEM((MLA_HEADS, ta, 1), F32), pltpu.VMEM((MLA_HEADS, ta, 1), F32),
                        pltpu.VMEM((MLA_HEADS, ta, LANES), F32)],
        compiler_params=_cparams(("parallel", "arbitrary", "arbitrary")),
        name="mla_prompt_attn",
    )(q, k, v)


def _moba_prompt_kernel(q_ref, k_ref, v_ref, km_ref, o_ref, bias_sc, m_sc, l_sc, acc_sc, *, nblk):
    qi, ki = pl.program_id(1), pl.program_id(2)
    tq = q_ref.shape[1]
    lane = lax.broadcasted_iota(I32, (tq, LANES), 1)

    def head_q(h):
        qp = q_ref[0, :, (h // 2) * LANES:(h // 2 + 1) * LANES]
        return jnp.where((lane // MOBA_HD) == (h % 2), qp, jnp.zeros_like(qp))

    @pl.when(ki == 0)
    def _():
        m_sc[...] = jnp.full(m_sc.shape, NEG, F32)
        l_sc[...] = jnp.zeros(l_sc.shape, F32)
        acc_sc[...] = jnp.zeros(acc_sc.shape, F32)
        for h in range(MOBA_HEADS):
            kmp = km_ref[0, :, (h // 2) * LANES:(h // 2 + 1) * LANES]
            gate = lax.dot_general(head_q(h).astype(F32), kmp, NT_DIMS, preferred_element_type=F32,
                                   precision=lax.Precision.HIGHEST)
            gate = jnp.where(lane < qi, gate, -jnp.inf)
            rank = jnp.zeros((tq, LANES), I32)
            for i in range(nblk):
                c = gate[:, i:i + 1]
                beats = jnp.logical_or(c > gate, jnp.logical_and(c == gate, lane > i))
                rank = rank + beats.astype(I32)
            sel = jnp.logical_and(rank < MOBA_TOPK, lane < qi)
            bias_sc[h] = jnp.where(sel, 0.0, NEG)

    def attend(mask_fn):
        for h in range(MOBA_HEADS):
            kp = k_ref[0, :, (h // 2) * LANES:(h // 2 + 1) * LANES].astype(BF16)
            s = lax.dot_general(head_q(h), kp, NT_DIMS, preferred_element_type=F32) * MOBA_SCALE
            vp = v_ref[0, :, (h // 2) * LANES:(h // 2 + 1) * LANES].astype(BF16)
            _online_update(mask_fn(s, h), m_sc, l_sc, acc_sc, vp, idx=h)

    @pl.when(ki < qi)
    def _():
        def past(s, h):
            colb = jnp.sum(jnp.where(lane == ki, bias_sc[h], 0.0), axis=1, keepdims=True)
            return s + colb
        attend(past)

    @pl.when(ki == qi)
    def _():
        row = lax.broadcasted_iota(I32, (tq, tq), 0)
        col = lax.broadcasted_iota(I32, (tq, tq), 1)
        attend(lambda s, h: jnp.where(col <= row, s, NEG))
        outs = []
        for h in range(MOBA_HEADS):
            o = acc_sc[h] / l_sc[h]
            outs.append(o[:, (h % 2) * MOBA_HD:(h % 2 + 1) * MOBA_HD])
        o_ref[0] = jnp.concatenate(outs, axis=1).astype(o_ref.dtype)


def _moba_prompt(q, k, v, kmean):
    b, t, w = q.shape
    assert t % MOBA_BLOCK == 0
    nblk = t // MOBA_BLOCK
    tb = MOBA_BLOCK
    return pl.pallas_call(
        functools.partial(_moba_prompt_kernel, nblk=nblk),
        grid=(b, nblk, nblk),
        in_specs=[pl.BlockSpec((1, tb, w), lambda bi, qi, ki: (bi, qi, 0)),
                  pl.BlockSpec((1, tb, w), lambda bi, qi, ki: (bi, jnp.minimum(ki, qi), 0)),
                  pl.BlockSpec((1, tb, w), lambda bi, qi, ki: (bi, jnp.minimum(ki, qi), 0)),
                  pl.BlockSpec((1, LANES, w), lambda bi, qi, ki: (bi, 0, 0))],
        out_specs=pl.BlockSpec((1, tb, w), lambda bi, qi, ki: (bi, qi, 0)),
        out_shape=jax.ShapeDtypeStruct((b, t, w), BF16),
        scratch_shapes=[pltpu.VMEM((MOBA_HEADS, tb, LANES), F32), pltpu.VMEM((MOBA_HEADS, tb, 1), F32),
                        pltpu.VMEM((MOBA_HEADS, tb, 1), F32), pltpu.VMEM((MOBA_HEADS, tb, LANES), F32)],
        compiler_params=_cparams(("parallel", "arbitrary", "arbitrary")),
        name="moba_prompt_attn",
    )(q, k, v, kmean)


MLA_PAGES_PER_STEP = 8


def _mla_sample_kernel(pt_ref, ql_ref, qr_ref, *rest):
    npp = MLA_PAGES_PER_STEP
    lat_refs, kpe_refs = rest[:npp], rest[npp:2 * npp]
    latn_ref, kpen_ref, wuv_ref, o_ref, m_sc, l_sc, acc_sc = rest[2 * npp:]
    step = pl.program_id(1)
    rows = ql_ref.shape[1]
    nq = rows // MLA_HEADS

    @pl.when(step == 0)
    def _():
        m_sc[...] = jnp.full(m_sc.shape, NEG, F32)
        l_sc[...] = jnp.zeros(l_sc.shape, F32)
        acc_sc[...] = jnp.zeros(acc_sc.shape, F32)

    ql, qr = ql_ref[0], qr_ref[0]

    def scores(lat, kpe_t):
        s = lax.dot_general(ql, lat, NT_DIMS, preferred_element_type=F32)
        s = s + jnp.dot(qr, kpe_t, preferred_element_type=F32)
        return s * MLA_SCALE

    def pv(p, lat):
        return jnp.dot(p, lat, preferred_element_type=F32)

    lats = [lat_refs[j][0].astype(BF16) for j in range(npp)]
    ss = [scores(lats[j], kpe_refs[j][0].astype(BF16)) for j in range(npp)]
    _online_update_multi(ss, lats, m_sc, l_sc, acc_sc, pv)

    @pl.when(step == pl.num_programs(1) - 1)
    def _():
        latn = latn_ref[0].astype(BF16)
        s = scores(latn, kpen_ref[0].astype(BF16))
        row = lax.broadcasted_iota(I32, s.shape, 0)
        col = lax.broadcasted_iota(I32, s.shape, 1)
        s = jnp.where(col <= (row % nq), s, NEG)
        _online_update_multi([s], [latn], m_sc, l_sc, acc_sc, pv)
        on = (acc_sc[...] / l_sc[...]).astype(BF16)
        t = jnp.dot(on, wuv_ref[...], preferred_element_type=F32)
        r2 = lax.broadcasted_iota(I32, t.shape, 0)
        c2 = lax.broadcasted_iota(I32, t.shape, 1)
        t = jnp.where((c2 // MLA_V) == (r2 // nq), t, 0.0)
        out = t[0:nq]
        for h in range(1, MLA_HEADS):
            out = out + t[h * nq:(h + 1) * nq]
        o_ref[0] = out.astype(o_ref.dtype)


def _mla_sample(page_table, qlat, qrope, lat_pool, kpe_pool, lat_new, kpe_new, wuv):
    b, rows, c = qlat.shape
    n_pages = page_table.shape[1]
    npp = MLA_PAGES_PER_STEP
    assert n_pages % npp == 0
    nq = rows // MLA_HEADS

    def page_spec(shape, j):
        return pl.BlockSpec((1,) + shape, lambda bi, s, pt: (pt[bi, s * npp + j], 0, 0))

    def bspec(shape):
        return pl.BlockSpec((1,) + shape, lambda bi, s, pt: (bi, 0, 0))

    grid_spec = pltpu.PrefetchScalarGridSpec(
        num_scalar_prefetch=1,
        grid=(b, n_pages // npp),
        in_specs=[bspec((rows, c)), bspec((rows, MLA_ROPE))]
        + [page_spec((PAGE_SIZE, c), j) for j in range(npp)]
        + [page_spec((MLA_ROPE, PAGE_SIZE), j) for j in range(npp)]
        + [bspec(lat_new.shape[1:]), bspec(kpe_new.shape[1:]),
           pl.BlockSpec(wuv.shape, lambda bi, s, pt: (0, 0))],
        out_specs=pl.BlockSpec((1, nq, wuv.shape[1]), lambda bi, s, pt: (bi, 0, 0)),
        scratch_shapes=[pltpu.VMEM((rows, 1), F32), pltpu.VMEM((rows, 1), F32), pltpu.VMEM((rows, c), F32)],
    )
    return pl.pallas_call(
        _mla_sample_kernel,
        grid_spec=grid_spec,
        out_shape=jax.ShapeDtypeStruct((b, nq, wuv.shape[1]), BF16),
        compiler_params=_cparams(("parallel", "arbitrary")),
        name="mla_sample_attn",
    )(page_table, qlat, qrope, *([lat_pool] * npp), *([kpe_pool] * npp), lat_new, kpe_new, wuv)


MOBA_BLOCKS_PER_STEP = 2


def _moba_sample_kernel(pt_ref, q_ref, *rest, nblk, nq):
    npg = MOBA_BLOCKS_PER_STEP * PAGES_PER_BLOCK
    k_refs, v_refs = rest[:npg], rest[npg:2 * npg]
    kn_ref, vn_ref, o_ref, gate_sc, m_sc, l_sc, acc_sc = rest[2 * npg:]
    step = pl.program_id(1)
    q = q_ref[0]
    rows = q.shape[0]
    lane = lax.broadcasted_iota(I32, (rows, LANES), 1)

    def pv(p, vt):
        return lax.dot_general(p, vt, NT_DIMS, preferred_element_type=F32)

    @pl.when(step == 0)
    def _():
        gate_sc[...] = jnp.zeros(gate_sc.shape, F32)
        m_sc[...] = jnp.zeros(m_sc.shape, F32)
        l_sc[...] = jnp.zeros(l_sc.shape, F32)

    gate, m_all_blk, l_all_blk = gate_sc[...], m_sc[...], l_sc[...]
    for i in range(MOBA_BLOCKS_PER_STEP):
        n = step * MOBA_BLOCKS_PER_STEP + i
        pages = range(i * PAGES_PER_BLOCK, (i + 1) * PAGES_PER_BLOCK)
        ss = []
        gsum = jnp.zeros((rows, 1), F32)
        for j in pages:
            kt = k_refs[j][0].reshape(MOBA_WIDTH, PAGE_SIZE).astype(BF16)
            s = jnp.dot(q, kt, preferred_element_type=F32)
            gsum = gsum + jnp.sum(s, axis=1, keepdims=True)
            ss.append(s * MOBA_SCALE)
        mb = jnp.max(ss[0], axis=1, keepdims=True)
        for s in ss[1:]:
            mb = jnp.maximum(mb, jnp.max(s, axis=1, keepdims=True))
        lb = jnp.zeros((rows, 1), F32)
        ob = jnp.zeros((rows, MOBA_WIDTH), F32)
        for s, j in zip(ss, pages):
            p = jnp.exp(s - mb)
            lb = lb + jnp.sum(p, axis=1, keepdims=True)
            ob = ob + pv(p.astype(BF16), v_refs[j][0].reshape(MOBA_WIDTH, PAGE_SIZE).astype(BF16))
        hit = lane == n
        gate = jnp.where(hit, gsum * (1.0 / MOBA_BLOCK), gate)
        m_all_blk = jnp.where(hit, mb, m_all_blk)
        l_all_blk = jnp.where(hit, lb, l_all_blk)
        acc_sc[n] = ob
    gate_sc[...], m_sc[...], l_sc[...] = gate, m_all_blk, l_all_blk

    @pl.when(step == pl.num_programs(1) - 1)
    def _():
        gate = jnp.where(lane < nblk, gate_sc[...], -jnp.inf)
        chosen = jnp.zeros(gate.shape, jnp.bool_)
        for _ in range(min(MOBA_TOPK, nblk)):
            mx = jnp.max(gate, axis=1, keepdims=True)
            first = jnp.min(jnp.where(gate == mx, lane, LANES), axis=1, keepdims=True)
            sel = lane == first
            chosen = jnp.logical_or(chosen, sel)
            gate = jnp.where(sel, -jnp.inf, gate)
        s_own = jnp.dot(q, kn_ref[0].astype(BF16), preferred_element_type=F32) * MOBA_SCALE
        r = lax.broadcasted_iota(I32, s_own.shape, 0)
        s_own = jnp.where(lane <= (r % nq), s_own, NEG)
        m_own = jnp.max(s_own, axis=1, keepdims=True)
        p_own = jnp.exp(s_own - m_own)
        m_all = jnp.maximum(jnp.max(jnp.where(chosen, m_sc[...], NEG), axis=1, keepdims=True), m_own)
        w = jnp.where(chosen, jnp.exp(jnp.where(chosen, m_sc[...], m_all) - m_all), 0.0)
        w_own = jnp.exp(m_own - m_all)
        l_all = jnp.sum(w * l_sc[...], axis=1, keepdims=True) + w_own * jnp.sum(p_own, axis=1, keepdims=True)
        acc = w_own * pv(p_own.astype(BF16), vn_ref[0].astype(BF16))

        def body(i, acc):
            wi = jnp.sum(jnp.where(lane == i, w, 0.0), axis=1, keepdims=True)
            return acc + wi * acc_sc[i]

        acc = lax.fori_loop(0, nblk, body, acc)
        out = acc / l_all
        r2 = lax.broadcasted_iota(I32, out.shape, 0)
        c2 = lax.broadcasted_iota(I32, out.shape, 1)
        out = jnp.where((c2 // MOBA_HD) == (r2 // nq), out, 0.0)
        o = out[0:nq]
        for h in range(1, MOBA_HEADS):
            o = o + out[h * nq:(h + 1) * nq]
        o_ref[0] = o.astype(o_ref.dtype)


def _moba_sample(page_table, qbd, kt_pool, vt_pool, kt_new, vt_new):
    b, rows, w = qbd.shape
    ppb = PAGES_PER_BLOCK
    nblk = page_table.shape[1] // ppb
    npg = MOBA_BLOCKS_PER_STEP * ppb
    assert nblk <= LANES and nblk % MOBA_BLOCKS_PER_STEP == 0
    nq = rows // MOBA_HEADS

    def page_spec(j):
        return pl.BlockSpec((1, MOBA_HEADS, MOBA_HD, PAGE_SIZE), lambda bi, n, pt: (pt[bi, n * npg + j], 0, 0, 0))

    def bspec(shape):
        nd = len(shape)
        return pl.BlockSpec((1,) + shape, lambda bi, n, pt: (bi,) + (0,) * nd)

    grid_spec = pltpu.PrefetchScalarGridSpec(
        num_scalar_prefetch=1,
        grid=(b, nblk // MOBA_BLOCKS_PER_STEP),
        in_specs=[bspec((rows, w))]
        + [page_spec(j) for j in range(npg)] + [page_spec(j) for j in range(npg)]
        + [bspec(kt_new.shape[1:]), bspec(vt_new.shape[1:])],
        out_specs=bspec((nq, w)),
        scratch_shapes=[pltpu.VMEM((rows, LANES), F32), pltpu.VMEM((rows, LANES), F32),
                        pltpu.VMEM((rows, LANES), F32), pltpu.VMEM((nblk, rows, w), F32)],
    )
    return pl.pallas_call(
        functools.partial(_moba_sample_kernel, nblk=nblk, nq=nq),
        grid_spec=grid_spec,
        out_shape=jax.ShapeDtypeStruct((b, nq, w), BF16),
        compiler_params=_cparams(("parallel", "arbitrary")),
        name="moba_sample_attn",
    )(page_table, qbd, *([kt_pool] * npg), *([vt_pool] * npg), kt_new, vt_new)


def _post_kernel(om_ref, ob_ref, sg_ref, x_ref, gt_ref, sc_ref, sh_ref, wom_ref, wob_ref, wout_ref,
                 gf_ref, wpq_ref, x1_ref, hf_ref, qp_ref):
    o_mla = jnp.dot(om_ref[0], wom_ref[...], preferred_element_type=F32)
    o_mb = jnp.dot(ob_ref[0], wob_ref[...], preferred_element_type=F32)
    sg = sg_ref[0].astype(F32)
    merged = sg[:, :D_MODEL] * o_mla + sg[:, D_MODEL:] * o_mb
    y = jnp.dot(merged.astype(BF16), wout_ref[...], preferred_element_type=F32)
    x1 = x_ref[0] + gt_ref[0] * y
    x1_ref[0] = x1
    hf = _rms(x1, gf_ref[...]) * (1.0 + sc_ref[0]) + sh_ref[0]
    hf_ref[0] = hf
    qp_ref[0] = jnp.dot(hf.astype(BF16), wpq_ref[...], preferred_element_type=F32).astype(BF16)


def _post(om, ob, sg, x, gt, sc, sh, wom, wob, wout, gf, wpq):
    g, tg, d = x.shape
    tt = min(TOK_TILE, tg)
    nq = wpq.shape[1]
    return pl.pallas_call(
        _post_kernel,
        grid=(g, tg // tt),
        in_specs=[_xspec(tt, om.shape[2]), _xspec(tt, ob.shape[2]), _xspec(tt, sg.shape[2]), _xspec(tt, d),
                  _modspec(gt, tt), _modspec(sc, tt), _modspec(sh, tt),
                  _wspec(wom.shape), _wspec(wob.shape), _wspec(wout.shape), _wspec(gf.shape), _wspec(wpq.shape)],
        out_specs=[_xspec(tt, d), _xspec(tt, d), _xspec(tt, nq)],
        out_shape=[jax.ShapeDtypeStruct((g, tg, d), F32), jax.ShapeDtypeStruct((g, tg, d), F32),
                   jax.ShapeDtypeStruct((g, tg, nq), BF16)],
        compiler_params=_cparams(("parallel", "arbitrary")),
        name="post_attn",
    )(om, ob, sg, x, gt, sc, sh, wom, wob, wout, gf, wpq)


def _topk_rows(s, k, val_sc, idx_sc, payload=None):
    nrow = s.shape[0]
    rid = lax.broadcasted_iota(I32, s.shape, 0)

    def body(r, s):
        mx = jnp.max(s, axis=0, keepdims=True)
        first = jnp.min(jnp.where(s == mx, rid, nrow), axis=0, keepdims=True)
        hit = rid == first
        val_sc[pl.ds(r, 1), :] = mx
        if payload is None:
            idx_sc[pl.ds(r, 1), :] = first
        else:
            idx_sc[pl.ds(r, 1), :] = jnp.max(jnp.where(hit, payload, -1), axis=0, keepdims=True)
        return jnp.where(hit, -jnp.inf, s)

    lax.fori_loop(0, k, body, s)


def _router_kernel(qp_ref, sk_ref, eid_ref, g_ref, v1_sc, i1_sc, v2_sc, i2_sc, vb_sc, ib_sc, e_sc, w_sc):
    kk = PEER_TOPK
    half = PEER_DKEY // 2
    for h in range(PEER_HEADS):
        tops = []
        for p, (v_sc, i_sc) in enumerate(((v1_sc, i1_sc), (v2_sc, i2_sc))):
            qhp = qp_ref[:, (2 * h + p) * half:(2 * h + p + 1) * half]
            s = lax.dot_general(sk_ref[h, p], qhp, NT_DIMS, preferred_element_type=F32)
            _topk_rows(s, kk, v_sc, i_sc)
            tops.append((v_sc[...], i_sc[...]))
        (a, ia), (b, ib) = tops
        hk = kk // 2
        jr = lax.broadcasted_iota(I32, (hk, a.shape[1]), 0)
        cvs, cis = [], []
        for i in range(hk):
            nj = kk if i == 0 else hk
            cv = a[i:i + 1, :] + b[:nj]
            if kk // (i + 1) < nj:
                cv = jnp.where(jr < kk // (i + 1), cv, -jnp.inf)
            cvs.append(cv)
            cis.append(ia[i:i + 1, :] * PEER_NKEYS + ib[:nj])
        cvs.append(a[hk:] + b[0:1, :])
        cis.append(ia[hk:] * PEER_NKEYS + ib[0:1, :])
        _topk_rows(jnp.concatenate(cvs, axis=0), kk, vb_sc, ib_sc, payload=jnp.concatenate(cis, axis=0))
        best = vb_sc[...]
        ex = jnp.exp(best - jnp.max(best, axis=0, keepdims=True))
        w_sc[h * kk:(h + 1) * kk, :] = ex / jnp.sum(ex, axis=0, keepdims=True)
        e_sc[h * kk:(h + 1) * kk, :] = ib_sc[...]
    eid_ref[...] = e_sc[...].T
    g_ref[...] = w_sc[...].T


def _router(qp, sk):
    n, w = qp.shape
    tt = min(TOK_TILE, n)
    kk = PEER_TOPK
    nsel = PEER_HEADS * kk
    return pl.pallas_call(
        _router_kernel,
        grid=(n // tt,),
        in_specs=[pl.BlockSpec((tt, w), lambda i: (i, 0)),
                  pl.BlockSpec(sk.shape, lambda i: (0, 0, 0, 0))],
        out_specs=[pl.BlockSpec((tt, nsel), lambda i: (i, 0)), pl.BlockSpec((tt, nsel), lambda i: (i, 0))],
        out_shape=[jax.ShapeDtypeStruct((n, nsel), I32), jax.ShapeDtypeStruct((n, nsel), F32)],
        scratch_shapes=[pltpu.VMEM((kk, tt), F32), pltpu.VMEM((kk, tt), I32),
                        pltpu.VMEM((kk, tt), F32), pltpu.VMEM((kk, tt), I32),
                        pltpu.VMEM((kk, tt), F32), pltpu.VMEM((kk, tt), I32),
                        pltpu.VMEM((nsel, tt), I32), pltpu.VMEM((nsel, tt), F32)],
        compiler_params=_cparams(("parallel",)),
        name="peer_router",
    )(qp, sk)


PEER_SLOTS = 3


def _peer_kernel(eid_cur, eid_nxt, eid_far, g_ref, hf_ref, x1_ref, gt_ref, gfin_ref, uv_hbm, o_ref, buf, sem, *,
                 nsel):
    nt = pl.num_programs(1)
    step = pl.program_id(0) * nt + pl.program_id(1)
    total = pl.num_programs(0) * nt
    tg = hf_ref.shape[1]

    def row_copy(e, dst_row, s):
        return pltpu.make_async_copy(uv_hbm.at[pl.ds(e, 1)], buf.at[s, pl.ds(dst_row, 1)], sem.at[s])

    @pl.when(step == 0)
    def _():
        for s, eid_ref in ((0, eid_cur), (1, eid_nxt)):
            for tt in range(tg):
                def body(k, c, tt=tt, s=s, eid_ref=eid_ref):
                    row_copy(eid_ref[tt, k], tt * nsel + k, s).start()
                    return c
                lax.fori_loop(0, nsel, body, 0)

    def wait_slot(s):
        pltpu.make_async_copy(uv_hbm.at[pl.ds(0, tg * nsel)], buf.at[s], sem.at[s]).wait()

    def run(slot):
        far = (slot + 2) % PEER_SLOTS
        wait_slot(slot)
        eye = lax.broadcasted_iota(I32, (nsel, nsel), 0) == lax.broadcasted_iota(I32, (nsel, nsel), 1)
        ys = []
        for tt in range(tg):
            for k in range(nsel):
                row_copy(eid_far[tt, k], tt * nsel + k, far).start(priority=k % 2)
            rows = buf[slot, tt * nsel:(tt + 1) * nsel, :]
            u = lax.bitcast_convert_type(rows & jnp.uint32(0xFFFF0000), F32)
            v = lax.bitcast_convert_type(rows << 16, F32)
            a = jnp.sum(u * hf_ref[0, tt:tt + 1, :], axis=1, keepdims=True)
            act = 0.5 * a * (1.0 + lax.erf(a * (1.0 / math.sqrt(2.0))))
            grow = jnp.broadcast_to(g_ref[0, tt:tt + 1, :], (nsel, nsel))
            gcol = jnp.sum(jnp.where(eye, grow, 0.0), axis=1, keepdims=True)
            ys.append(jnp.sum((gcol * act) * v, axis=0, keepdims=True))
        y = jnp.concatenate(ys, axis=0)
        x2 = x1_ref[0] + gt_ref[0] * y
        o_ref[0] = _rms(x2, gfin_ref[...])

        @pl.when(step + 1 == total)
        def _():
            wait_slot((slot + 1) % PEER_SLOTS)
            wait_slot(far)

    for slot in range(PEER_SLOTS):
        pl.when(step % PEER_SLOTS == slot)(functools.partial(run, slot))


def _peer(eid, gw, hf, x1, gt, gfin, uv):
    g, tg, d = hf.shape
    n, nsel = eid.shape
    tp = PEER_TOK
    nt = tg // tp
    last = g * nt - 1
    gw3 = gw.reshape(g, tg, nsel)
    return pl.pallas_call(
        functools.partial(_peer_kernel, nsel=nsel),
        grid=(g, nt),
        in_specs=[pl.BlockSpec((tp, nsel), lambda gi, t, a=a: (jnp.minimum(gi * nt + t + a, last), 0),
                               memory_space=pltpu.SMEM) for a in range(PEER_SLOTS)]
        + [_xspec(tp, nsel), _xspec(tp, d), _xspec(tp, d), _modspec(gt, tp), _wspec(gfin.shape),
           pl.BlockSpec(memory_space=pl.ANY)],
        out_specs=_xspec(tp, d),
        out_shape=jax.ShapeDtypeStruct((g, tg, d), F32),
        scratch_shapes=[pltpu.VMEM((PEER_SLOTS, tp * nsel, uv.shape[1]), uv.dtype),
                        pltpu.SemaphoreType.DMA((PEER_SLOTS,))],
        compiler_params=_cparams(("arbitrary", "arbitrary")),
        name="peer_experts",
    )(*([eid] * PEER_SLOTS), gw3, hf, x1, gt, gfin, uv)


def _rope_tables(pos, half, lane_off, period):
    inv = jnp.exp(-math.log(ROPE_THETA) * jnp.arange(half, dtype=F32) / half)
    ang = pos.astype(F32)[:, None] * inv[None, :]
    cos, sin = jnp.cos(ang), jnp.sin(ang)
    t = pos.shape[0]
    ones = lambda n: jnp.ones((t, n), F32)
    zeros = lambda n: jnp.zeros((t, n), F32)
    tail = period - lane_off - 2 * half
    c = jnp.concatenate([ones(lane_off), cos, cos, ones(tail)], axis=1)
    a = jnp.concatenate([zeros(lane_off), -sin, zeros(half + tail)], axis=1)
    b = jnp.concatenate([zeros(lane_off + half), sin, zeros(tail)], axis=1)
    rep = LANES // period
    return jnp.stack([jnp.tile(c, (1, rep)), jnp.tile(a, (1, rep)), jnp.tile(b, (1, rep))])


def _prep_weights(w_in, w_uq, w_uk, w_uv):
    d = w_in.shape[0]
    sizes = (MLA_Q_LORA, MLA_KV_LORA, MLA_ROPE, MOBA_WIDTH, MOBA_WIDTH, MOBA_WIDTH, D_MODEL, D_MODEL)
    offs = [0]
    for s in sizes:
        offs.append(offs[-1] + s)
    seg = [w_in[:, offs[i]:offs[i + 1]] for i in range(len(sizes))]
    kr = jnp.zeros((d, LANES), w_in.dtype).at[:, KR_LANE:KR_LANE + MLA_ROPE].set(seg[2])
    w_in_p = jnp.concatenate([seg[0], seg[1], kr] + seg[3:], axis=1).astype(BF16)
    hd = MLA_NOPE + MLA_ROPE
    w_uq_p = jnp.pad(w_uq, ((0, 0), (0, 0), (0, LANES - hd))).reshape(w_uq.shape[0], MLA_HEADS * LANES).astype(BF16)
    wuk_p = jnp.pad(w_uk.transpose(1, 0, 2), ((0, 0), (0, 0), (0, LANES - MLA_NOPE)))
    wuk_p = wuk_p.reshape(MLA_KV_LORA, MLA_HEADS * LANES).astype(BF16)
    wukt_p = jnp.pad(w_uk.transpose(0, 2, 1), ((0, 0), (0, LANES - MLA_NOPE), (0, 0))).astype(BF16)
    wuv_all = w_uv.transpose(1, 0, 2).reshape(MLA_KV_LORA, MLA_HEADS * MLA_V).astype(BF16)
    return w_in_p, w_uq_p, wuk_p, wukt_p, wuv_all


def _pad_rows(x, n):
    return jnp.pad(x, ((0, 0), (0, n - x.shape[1])) + ((0, 0),) * (x.ndim - 2))


def kernel(x_prompt, x_sample, cache_mla_latent, cache_mla_kpe, cache_moba_k, cache_moba_v, page_table,
           c_prompt, c_sample, w_ada, b_ada, g_norm_mix, g_norm_ffn, w_in, g_q_lora, w_uq, g_kv_lora, w_uk, w_uv,
           w_o_mla, w_o_moba, w_out, w_peer_q, peer_sub_keys, peer_u, peer_v, g_norm_final):
    assert w_ada.shape[0] == 1, "one trunk layer"
    bp, tp, d = x_prompt.shape
    bs, ts, _ = x_sample.shape
    n_pool = cache_mla_latent.shape[1]
    n_pages = page_table.shape[1]
    past_len = n_pages * PAGE_SIZE
    assert past_len % MOBA_BLOCK == 0, "the sample group's own MoBA block holds only the new tokens"
    nblk_s = past_len // MOBA_BLOCK
    ns = bs * ts

    w_in_p, w_uq_p, wuk_p, wukt_p, wuv_all = _prep_weights(w_in[0], w_uq[0], w_uk[0], w_uv[0])
    w_ada_b = w_ada[0].astype(BF16)
    wom, wob, wout, wpq = (w[0].astype(BF16) for w in (w_o_mla, w_o_moba, w_out, w_peer_q))
    sk = peer_sub_keys[0].astype(BF16)
    def bf16_bits(t):
        return lax.bitcast_convert_type(t.astype(BF16), jnp.uint16).astype(jnp.uint32)
    uv = (bf16_bits(peer_u[0]) << 16) | bf16_bits(peer_v[0])
    gm, gf, gq, gkv = g_norm_mix[0][None], g_norm_ffn[0][None], g_q_lora[0][None], g_kv_lora[0][None]
    gfin = g_norm_final[None]

    mod = _adaln(jnp.concatenate([c_prompt, c_sample], axis=0), w_ada_b, b_ada[0][None])
    mods_p = [m[:, None, :] for m in jnp.split(mod[:bp], 6, axis=-1)]
    mods_s = [jnp.repeat(m, ts, axis=0)[None] for m in jnp.split(mod[bp:], 6, axis=-1)]

    pos_p = jnp.arange(tp, dtype=I32)
    pos_s = jnp.tile(past_len + jnp.arange(ts, dtype=I32), bs)

    def trunk(x, mods, pos, attend):
        sh_a, sc_a, gt_a, sh_f, sc_f, gt_f = mods
        g, tg, _ = x.shape
        tq = _rope_tables(pos, MLA_ROPE // 2, KR_LANE, LANES)
        tm = _rope_tables(pos, ROT_DIMS // 2, 0, MOBA_HD)
        qn, kvlat, kpe, qmb, kmb, vmb, sg, kmean = _inproj(x, sh_a, sc_a, gm, w_in_p, gq, gkv, tq, tm)
        q = _qproj(qn, w_uq_p, tq)
        o_mla, o_mb = attend(q, kvlat, kpe, qmb, kmb, vmb, kmean)
        x1, hf, qp = _post(o_mla, o_mb, sg, x, gt_a, sc_f, sh_f, wom, wob, wout, gf, wpq)
        eid, gw = _router(qp.reshape(g * tg, -1), sk)
        y = _peer(eid, gw, hf, x1, gt_f, gfin, uv)
        return y, (kvlat, kpe[:, :, KR_LANE:KR_LANE + MLA_ROPE], kmb, vmb)

    def attend_prompt(q, kvlat, kpe, qmb, kmb, vmb, kmean):
        k, v = _kvprep(kvlat, kpe, wuk_p, wuv_all)
        o_mla = _mla_prompt(q, k, v)
        km = _pad_rows(kmean.reshape(kmean.shape[0], kmean.shape[1], MOBA_WIDTH), LANES)
        o_mb = _moba_prompt(qmb, kmb, vmb, km)
        return o_mla, o_mb

    lat_pool = cache_mla_latent.reshape(n_pool, PAGE_SIZE, MLA_KV_LORA)
    kpe_pool = jnp.transpose(cache_mla_kpe[0], (0, 2, 1))
    kt_pool = jnp.transpose(cache_moba_k[0], (0, 2, 3, 1))
    vt_pool = jnp.transpose(cache_moba_v[0], (0, 2, 3, 1))

    def key_minor(a):
        return _pad_rows(a, LANES).transpose(0, 2, 1)

    def attend_sample(q, kvlat, kpe, qmb, kmb, vmb, kmean):
        del kmean
        def head_major(a, width):
            return a.reshape(bs, ts, MLA_HEADS, width).transpose(0, 2, 1, 3).reshape(bs, MLA_HEADS * ts, width)
        qlat = head_major(_qlat(q, wukt_p), MLA_KV_LORA)
        qrope = head_major(q.reshape(1, ns, MLA_HEADS, LANES)[..., KR_LANE:KR_LANE + MLA_ROPE], MLA_ROPE)
        lat_new = _pad_rows(kvlat.reshape(bs, ts, MLA_KV_LORA), LANES)
        kpe_new = key_minor(kpe.reshape(bs, ts, LANES)[..., KR_LANE:KR_LANE + MLA_ROPE])
        o_mla = _mla_sample(page_table, qlat, qrope, lat_pool, kpe_pool, lat_new, kpe_new, wuv_all)
        qm = qmb.reshape(bs, ts, MOBA_HEADS, MOBA_HD).transpose(0, 2, 1, 3)
        eye = jnp.eye(MOBA_HEADS, dtype=qm.dtype)
        qbd = (qm[:, :, :, None, :] * eye[None, :, None, :, None]).reshape(bs, MOBA_HEADS * ts, MOBA_WIDTH)
        o_mb = _moba_sample(page_table, qbd, kt_pool, vt_pool,
                            key_minor(kmb.reshape(bs, ts, MOBA_WIDTH)), key_minor(vmb.reshape(bs, ts, MOBA_WIDTH)))
        return o_mla.reshape(1, ns, MLA_HEADS * MLA_V), o_mb.reshape(1, ns, MOBA_WIDTH)

    y_p, st_p = trunk(x_prompt, mods_p, pos_p, attend_prompt)
    y_s, st_s = trunk(x_sample.reshape(1, ns, d), mods_s, pos_s, attend_sample)

    def states(st, b, t):
        lat, kpe, kmb, vmb = st
        return (lat.reshape(1, b, t, MLA_KV_LORA), kpe.reshape(1, b, t, MLA_ROPE),
                kmb.reshape(1, b, t, MOBA_HEADS, MOBA_HD), vmb.reshape(1, b, t, MOBA_HEADS, MOBA_HD))

    return (y_p, y_s.reshape(bs, ts, d)) + states(st_p, bp, tp) + states(st_s, bs, ts)
```
